```python
import math
import jax, jax.numpy as jnp
from jax import lax
import numpy as np

D_MODEL = 1024
BATCH = 4
SEQ = 4096
DEPTH = 4
DEC_BATCH = 128
DEC_SEQ = 8
PAST_LEN = 8192
PAGE_SIZE = 128

HEAD_DIM = 64
SCALE = HEAD_DIM ** -0.5
ROT_DIM = HEAD_DIM // 4
ROPE_THETA = 500000.0
NORM_EPS = 1e-6
SUBLN_EPS = 1e-5
A_HEADS = 4
A_KV_HEADS = 2
A_GROUP = A_HEADS // A_KV_HEADS
B_HEADS = 8
B_KV_HEADS = 4
B_GROUP = B_HEADS // B_KV_HEADS
MOBA_BLOCK = 256
MOBA_TOPK = 3
MOBA_Q_CHUNK = 32
C_HEADS = 16
C_KV_HEADS = 2
C_GROUP = C_HEADS // C_KV_HEADS
WINDOW = 128
Q_BLOCK = 128
D_FF = 2816
CONV_W = 3

G_SIZES = [A_HEADS * 2 * HEAD_DIM, A_KV_HEADS * 2 * HEAD_DIM, A_KV_HEADS * 2 * HEAD_DIM,
           B_HEADS * HEAD_DIM, B_KV_HEADS * HEAD_DIM, B_KV_HEADS * HEAD_DIM]
G_IN = sum(G_SIZES)
G_OUT = A_HEADS * 2 * HEAD_DIM + B_HEADS * HEAD_DIM
L_SIZES = [C_HEADS * HEAD_DIM, C_KV_HEADS * HEAD_DIM, C_KV_HEADS * HEAD_DIM]
L_IN = sum(L_SIZES)
L_OUT = C_HEADS * HEAD_DIM
N_GLOBAL = (DEPTH + 1) // 2
N_LOCAL = DEPTH // 2

kernel_name = 'hybrid_diff_moba_swa_convffn_step'


def rmsnorm(x, g, eps=NORM_EPS):
    xf = x.astype(jnp.float32)
    y = xf * lax.rsqrt(jnp.mean(xf * xf, axis=-1, keepdims=True) + eps) * g.astype(jnp.float32)
    return y.astype(x.dtype)


def split_cols(y, sizes):
    return jnp.split(y, [int(c) for c in np.cumsum(sizes)[:-1]], axis=-1)


def rope(x, pos):
    half = ROT_DIM // 2
    inv = ROPE_THETA ** (-jnp.arange(half, dtype=jnp.float32) * 2.0 / ROT_DIM)
    ang = pos.astype(jnp.float32)[:, None] * inv[None, :]
    ang = ang.reshape(ang.shape[:1] + (1,) * (x.ndim - 3) + (half,))
    cos, sin = jnp.cos(ang), jnp.sin(ang)
    xf = x.astype(jnp.float32)
    x1, x2, rest = xf[..., :half], xf[..., half:ROT_DIM], xf[..., ROT_DIM:]
    return jnp.concatenate([x1 * cos - x2 * sin, x2 * cos + x1 * sin, rest], axis=-1).astype(x.dtype)


def map_query_blocks(fn, q, pos, size):
    B, S = q.shape[:2]
    n = S // size
    qs = jnp.moveaxis(q.reshape((B, n, size) + q.shape[2:]), 1, 0)
    out = lax.map(lambda a: fn(a[0], a[1]), (qs, pos.reshape(n, size)))
    return jnp.moveaxis(out, 0, 1).reshape(B, S, -1)


def diff_lambda_value(p, lam_init):
    p = p.astype(jnp.float32)
    return jnp.exp(jnp.sum(p[0] * p[1])) - jnp.exp(jnp.sum(p[2] * p[3])) + lam_init


def diff_attn(q, k, v, q_pos, k_pos, lam, lam_init, subln_g):
    s = jnp.einsum('btkgmd,blkmd->bkgmtl', q, k, preferred_element_type=jnp.float32) * SCALE
    mask = k_pos[None, :] <= q_pos[:, None]
    p = jax.nn.softmax(jnp.where(mask, s, -jnp.inf), axis=-1)
    p = p[:, :, :, 0] - lam * p[:, :, :, 1]
    o = jnp.einsum('bkgtl,blke->btkge', p.astype(v.dtype), v)
    o = rmsnorm(o, subln_g, SUBLN_EPS) * (1.0 - lam_init)
    return o.reshape(o.shape[0], o.shape[1], -1)


def moba_blocks(k, v):
    B, L, KV, HD = k.shape
    nb = -(-L // MOBA_BLOCK)
    pad = ((0, 0), (0, nb * MOBA_BLOCK - L), (0, 0), (0, 0))
    kb = jnp.pad(k, pad).reshape(B, nb, MOBA_BLOCK, KV, HD).transpose(0, 3, 1, 2, 4)
    vb = jnp.pad(v, pad).reshape(B, nb, MOBA_BLOCK, KV, HD).transpose(0, 3, 1, 2, 4)
    kmean = jnp.mean(kb.astype(jnp.float32), axis=3)
    return kb, vb, kmean


def moba_attn(q, q_pos, kb, vb, kmean):
    B, T, KV, G, _ = q.shape
    nb = kb.shape[2]
    n_top = min(MOBA_TOPK, nb)
    own = q_pos // MOBA_BLOCK
    gate = jnp.einsum('btkgd,bknd->btkgn', q.astype(jnp.float32), kmean)
    fully_past = jnp.arange(nb)[None, :] < own[:, None]
    gate = jnp.where(fully_past[None, :, None, None, :], gate, -jnp.inf)
    _, top = lax.top_k(gate, n_top)
    own_b = jnp.broadcast_to(own[None, :, None, None, None], (B, T, KV, G, 1)).astype(top.dtype)
    idx = jnp.concatenate([top, own_b], axis=-1)
    valid = jnp.concatenate([top < own_b, jnp.ones(own_b.shape, dtype=bool)], axis=-1)
    bi = jnp.arange(B)[:, None, None, None, None]
    hi = jnp.arange(KV)[None, None, :, None, None]
    ks = kb[bi, hi, idx]
    vs = vb[bi, hi, idx]
    kpos = idx[..., None] * MOBA_BLOCK + jnp.arange(MOBA_BLOCK)
    mask = valid[..., None] & (kpos <= q_pos[None, :, None, None, None, None])
    s = jnp.einsum('btkgd,btkgjld->btkgjl', q, ks, preferred_element_type=jnp.float32) * SCALE
    s = jnp.where(mask, s, -jnp.inf)
    p = jax.nn.softmax(s.reshape(s.shape[:4] + (-1,)), axis=-1).reshape(s.shape)
    o = jnp.einsum('btkgjl,btkgjld->btkgd', p.astype(vs.dtype), vs)
    return o.reshape(B, T, KV * G * HEAD_DIM)


def global_project(h, w_in, pos):
    B, T, _ = h.shape
    qa, ka, va, qb, kb, vb = split_cols(h @ w_in, G_SIZES)
    qa = rope(qa.reshape(B, T, A_KV_HEADS, A_GROUP, 2, HEAD_DIM), pos)
    ka = rope(ka.reshape(B, T, A_KV_HEADS, 2, HEAD_DIM), pos)
    va = va.reshape(B, T, A_KV_HEADS, 2 * HEAD_DIM)
    qb = rope(qb.reshape(B, T, B_KV_HEADS, B_GROUP, HEAD_DIM), pos)
    kb = rope(kb.reshape(B, T, B_KV_HEADS, HEAD_DIM), pos)
    vb = vb.reshape(B, T, B_KV_HEADS, HEAD_DIM)
    return qa, ka, va, qb, kb, vb


def global_prompt(h, w_in, w_out, lam, lam_init, subln_g):
    S = h.shape[1]
    pos = jnp.arange(S, dtype=jnp.int32)
    qa, ka, va, qb, kb, vb = global_project(h, w_in, pos)
    oa = map_query_blocks(lambda q, p: diff_attn(q, ka, va, p, pos, lam, lam_init, subln_g), qa, pos, Q_BLOCK)
    kbl, vbl, kmean = moba_blocks(kb, vb)
    ob = map_query_blocks(lambda q, p: moba_attn(q, p, kbl, vbl, kmean), qb, pos, MOBA_Q_CHUNK)
    y = jnp.concatenate([oa, ob], axis=-1) @ w_out
    return y, ka, va, kb, vb


def global_sample(h, cache_ak, cache_av, cache_bk, cache_bv, layer, page_table, past,
                  w_in, w_out, lam, lam_init, subln_g):
    T = h.shape[1]
    q_pos = past + jnp.arange(T, dtype=jnp.int32)
    k_pos = jnp.arange(past + T, dtype=jnp.int32)
    qa, ka, va, qb, kb, vb = global_project(h, w_in, q_pos)

    def one_seq(a):
        qa1, ka1, va1, qb1, kb1, vb1, pages = a

        def rows(cache, new):
            old = cache[layer, pages]
            return jnp.concatenate([old.reshape((past,) + old.shape[2:]), new], axis=0)[None]

        oa = diff_attn(qa1[None], rows(cache_ak, ka1), rows(cache_av, va1), q_pos, k_pos, lam, lam_init, subln_g)
        kbl, vbl, kmean = moba_blocks(rows(cache_bk, kb1), rows(cache_bv, vb1))
        ob = moba_attn(qb1[None], q_pos, kbl, vbl, kmean)
        return jnp.concatenate([oa, ob], axis=-1)[0]

    o = lax.map(one_seq, (qa, ka, va, qb, kb, vb, page_table))
    return o @ w_out, ka, va, kb, vb


def sink_window_attn(q, k, v, q_pos, k_pos, sinks):
    s = jnp.einsum('bntkgd,bnlkd->bnkgtl', q, k, preferred_element_type=jnp.float32) * SCALE
    rel = q_pos[:, :, None] - k_pos[:, None, :]
    mask = (rel >= 0) & (rel < WINDOW) & (k_pos[:, None, :] >= 0)
    s = jnp.where(mask[None, :, None, None], s, -jnp.inf)
    sink = jnp.broadcast_to(sinks.astype(jnp.float32).reshape(1, 1, C_KV_HEADS, C_GROUP, 1, 1), s.shape[:-1] + (1,))
    p = jax.nn.softmax(jnp.concatenate([s, sink], axis=-1), axis=-1)[..., :-1]
    o = jnp.einsum('bnkgtl,bnlkd->bntkgd', p.astype(v.dtype), v)
    B, N, T = q.shape[:3]
    return o.reshape(B, N * T, -1)


def local_project(h, w_in, pos):
    B, T, _ = h.shape
    q, k, v = split_cols(h @ w_in, L_SIZES)
    q = rope(q.reshape(B, T, C_KV_HEADS, C_GROUP, HEAD_DIM), pos)
    k = rope(k.reshape(B, T, C_KV_HEADS, HEAD_DIM), pos)
    return q, k, v.reshape(B, T, C_KV_HEADS, HEAD_DIM)


def local_prompt(h, w_in, w_out, sinks, buf):
    B, S, _ = h.shape
    pos = jnp.arange(S, dtype=jnp.int32)
    q, k, v = local_project(h, w_in, pos)
    nb = S // WINDOW

    def band(x):
        xb = jnp.pad(x, ((0, 0), (WINDOW, 0), (0, 0), (0, 0))).reshape((B, nb + 1, WINDOW) + x.shape[2:])
        return jnp.concatenate([xb[:, :-1], xb[:, 1:]], axis=2)

    k_pos = (jnp.arange(nb)[:, None] - 1) * WINDOW + jnp.arange(2 * WINDOW)[None, :]
    o = sink_window_attn(q.reshape((B, nb, WINDOW) + q.shape[2:]), band(k), band(v),
                         pos.reshape(nb, WINDOW), k_pos, sinks)
    return o @ w_out, k[:, S - buf:], v[:, S - buf:]


def local_sample(h, buf_k, buf_v, past, w_in, w_out, sinks):
    T = h.shape[1]
    buf = buf_k.shape[1]
    q_pos = past + jnp.arange(T, dtype=jnp.int32)
    q, k, v = local_project(h, w_in, q_pos)
    k_all = jnp.concatenate([buf_k, k], axis=1)
    v_all = jnp.concatenate([buf_v, v], axis=1)
    k_pos = past - buf + jnp.arange(buf + T, dtype=jnp.int32)
    o = sink_window_attn(q[:, None], k_all[:, None], v_all[:, None], q_pos[None], k_pos[None], sinks)
    return o @ w_out, k_all[:, -buf:], v_all[:, -buf:]


def conv_ffn(h, prev, w_up, conv_w, conv_b, w_down):
    u = h @ w_up
    up = jnp.concatenate([prev, u], axis=1)
    S = u.shape[1]
    c = conv_b
    for j in range(CONV_W):
        c = c + up[:, j:j + S] * conv_w[j]
    a, b = jnp.split(c, 2, axis=-1)
    y = (jax.nn.gelu(a, approximate=False) * b) @ w_down
    return y, up[:, up.shape[1] - (CONV_W - 1):]


def setup_inputs(seed: int = 0) -> dict:
    key = jax.random.key(seed)
    ks = jax.random.split(key, 32)
    n_pages = PAST_LEN // PAGE_SIZE
    n_pool = (DEC_BATCH * n_pages * 5) // 4
    buf = min(WINDOW, PAST_LEN)

    def nrm(k, shape, scale=1.0):
        return jax.random.normal(k, shape, jnp.float32) * scale

    page_table = jax.random.permutation(ks[10], n_pool)[:DEC_BATCH * n_pages].reshape(DEC_BATCH, n_pages).astype(jnp.int32)
    return {
        'x_prompt': nrm(ks[0], (BATCH, SEQ, D_MODEL)),
        'x_sample': nrm(ks[1], (DEC_BATCH, DEC_SEQ, D_MODEL)),
        'cache_a_k': nrm(ks[2], (N_GLOBAL, n_pool, PAGE_SIZE, A_KV_HEADS, 2, HEAD_DIM)),
        'cache_a_v': nrm(ks[3], (N_GLOBAL, n_pool, PAGE_SIZE, A_KV_HEADS, 2 * HEAD_DIM)),
        'cache_b_k': nrm(ks[4], (N_GLOBAL, n_pool, PAGE_SIZE, B_KV_HEADS, HEAD_DIM)),
        'cache_b_v': nrm(ks[5], (N_GLOBAL, n_pool, PAGE_SIZE, B_KV_HEADS, HEAD_DIM)),
        'state_c_k': nrm(ks[6], (N_LOCAL, DEC_BATCH, buf, C_KV_HEADS, HEAD_DIM)),
        'state_c_v': nrm(ks[7], (N_LOCAL, DEC_BATCH, buf, C_KV_HEADS, HEAD_DIM)),
        'state_ffn': nrm(ks[8], (DEPTH, DEC_BATCH, CONV_W - 1, 2 * D_FF)),
        'page_table': page_table,
        'norm_mix': 1.0 + nrm(ks[11], (DEPTH, D_MODEL), 0.02),
        'w_in_g': nrm(ks[12], (N_GLOBAL, D_MODEL, G_IN), D_MODEL ** -0.5),
        'w_out_g': nrm(ks[13], (N_GLOBAL, G_OUT, D_MODEL), G_OUT ** -0.5),
        'diff_lambda': nrm(ks[14], (N_GLOBAL, 4, HEAD_DIM), 0.1),
        'diff_subln': 1.0 + nrm(ks[15], (N_GLOBAL, 2 * HEAD_DIM), 0.02),
        'w_in_l': nrm(ks[16], (N_LOCAL, D_MODEL, L_IN), D_MODEL ** -0.5),
        'w_out_l': nrm(ks[17], (N_LOCAL, L_OUT, D_MODEL), L_OUT ** -0.5),
        'sinks': nrm(ks[18], (N_LOCAL, C_HEADS), 1.0),
        'norm_ffn': 1.0 + nrm(ks[19], (DEPTH, D_MODEL), 0.02),
        'w_up': nrm(ks[20], (DEPTH, D_MODEL, 2 * D_FF), D_MODEL ** -0.5),
        'conv_w': nrm(ks[21], (DEPTH, CONV_W, 2 * D_FF), CONV_W ** -0.5),
        'conv_b': nrm(ks[22], (DEPTH, 2 * D_FF), 0.02),
        'w_down': nrm(ks[23], (DEPTH, D_FF, D_MODEL), D_FF ** -0.5),
        'norm_final': 1.0 + nrm(ks[24], (D_MODEL,), 0.02),
    }


def reference(x_prompt, x_sample, cache_a_k, cache_a_v, cache_b_k, cache_b_v, state_c_k, state_c_v,
              state_ffn, page_table, norm_mix, w_in_g, w_out_g, diff_lambda, diff_subln, w_in_l, w_out_l,
              sinks, norm_ffn, w_up, conv_w, conv_b, w_down, norm_final):
    past = page_table.shape[1] * cache_a_k.shape[2]
    buf = state_c_k.shape[2]
    xp, xs = x_prompt, x_sample
    ak_p, av_p, bk_p, bv_p, ck_p, cv_p, f_p = [], [], [], [], [], [], []
    ak_s, av_s, bk_s, bv_s, ck_s, cv_s, f_s = [], [], [], [], [], [], []
    for l in range(DEPTH):
        i = l // 2
        hp, hs = rmsnorm(xp, norm_mix[l]), rmsnorm(xs, norm_mix[l])
        if l % 2 == 0:
            lam_init = 0.8 - 0.6 * math.exp(-0.3 * l)
            lam = diff_lambda_value(diff_lambda[i], lam_init)
            yp, ka, va, kb, vb = global_prompt(hp, w_in_g[i], w_out_g[i], lam, lam_init, diff_subln[i])
            ak_p.append(ka); av_p.append(va); bk_p.append(kb); bv_p.append(vb)
            ys, ka, va, kb, vb = global_sample(hs, cache_a_k, cache_a_v, cache_b_k, cache_b_v, i, page_table, past,
                                               w_in_g[i], w_out_g[i], lam, lam_init, diff_subln[i])
            ak_s.append(ka); av_s.append(va); bk_s.append(kb); bv_s.append(vb)
        else:
            yp, kc, vc = local_prompt(hp, w_in_l[i], w_out_l[i], sinks[i], buf)
            ck_p.append(kc); cv_p.append(vc)
            ys, kc, vc = local_sample(hs, state_c_k[i], state_c_v[i], past, w_in_l[i], w_out_l[i], sinks[i])
            ck_s.append(kc); cv_s.append(vc)
        xp = xp + yp
        xs = xs + ys
        hp, hs = rmsnorm(xp, norm_ffn[l]), rmsnorm(xs, norm_ffn[l])
        zero_prev = jnp.zeros((xp.shape[0], CONV_W - 1, 2 * D_FF), xp.dtype)
        fp, up = conv_ffn(hp, zero_prev, w_up[l], conv_w[l], conv_b[l], w_down[l])
        fs, us = conv_ffn(hs, state_ffn[l], w_up[l], conv_w[l], conv_b[l], w_down[l])
        f_p.append(up); f_s.append(us)
        xp = xp + fp
        xs = xs + fs
    return (rmsnorm(xp, norm_final), rmsnorm(xs, norm_final),
            jnp.stack(ak_p), jnp.stack(av_p), jnp.stack(bk_p), jnp.stack(bv_p),
            jnp.stack(ck_p), jnp.stack(cv_p), jnp.stack(f_p),
            jnp.stack(ak_s), jnp.stack(av_s), jnp.stack(bk_s), jnp.stack(bv_s),
            jnp.stack(ck_s), jnp.stack(cv_s), jnp.stack(f_s))
```

```python
import functools
import math

import jax
import jax.numpy as jnp
from jax import lax
from jax.experimental import pallas as pl
from jax.experimental.pallas import tpu as pltpu

HEAD_DIM = 64
SCALE = HEAD_DIM ** -0.5
ROT_DIM = HEAD_DIM // 4
ROPE_THETA = 500000.0
NORM_EPS = 1e-6
SUBLN_EPS = 1e-5
A_HEADS, A_KV_HEADS = 4, 2
A_GROUP = A_HEADS // A_KV_HEADS
B_HEADS, B_KV_HEADS = 8, 4
B_GROUP = B_HEADS // B_KV_HEADS
MOBA_BLOCK = 256
MOBA_TOPK = 3
C_HEADS, C_KV_HEADS = 16, 2
C_GROUP = C_HEADS // C_KV_HEADS
WINDOW = 128
CONV_W = 3

LANES = 128
SUBLANES = 8
HALF = LANES // 2
NEG = -1e30
MXU_DTYPE = jnp.bfloat16
VMEM_LIMIT = 52 * 1024 * 1024

F32 = jnp.float32


def _cparams(sem, vmem=None):
    return pltpu.CompilerParams(dimension_semantics=sem, vmem_limit_bytes=vmem)


def _tile(n, pref):
    t = min(n, pref)
    while n % t or t % SUBLANES:
        t -= 1
    return t


def _lane_iota(shape):
    return lax.broadcasted_iota(jnp.int32, shape, len(shape) - 1)


def _row_iota(shape):
    return lax.broadcasted_iota(jnp.int32, shape, len(shape) - 2)


def _half_place(chunk, src_half, dst_half):
    x = chunk if src_half == dst_half else pltpu.roll(chunk, HALF, axis=1)
    lane = _lane_iota(x.shape)
    keep = (lane >= HALF) if dst_half else (lane < HALF)
    return jnp.where(keep, x, 0.0)


def _merge_halves(lo, lo_src_half, hi, hi_src_half):
    a = lo if lo_src_half == 0 else pltpu.roll(lo, HALF, axis=1)
    b = hi if hi_src_half == 1 else pltpu.roll(hi, HALF, axis=1)
    return jnp.where(_lane_iota(a.shape) < HALF, a, b)


def _dot_nt(a, b):
    return lax.dot_general(a, b, (((1,), (1,)), ((), ())), preferred_element_type=F32)


def _dot(a, b):
    return jnp.dot(a, b, preferred_element_type=F32)


def _rms(x, g, eps):
    ms = jnp.mean(x * x, axis=-1, keepdims=True)
    return x * lax.rsqrt(ms + eps) * g


def _gelu(x):
    return 0.5 * x * (1.0 + lax.erf(x * math.sqrt(0.5)))


def _diff_lambda(lp, lam_init):
    a = jnp.sum(lp[0:1] * lp[1:2], axis=-1, keepdims=True)
    b = jnp.sum(lp[2:3] * lp[3:4], axis=-1, keepdims=True)
    return jnp.exp(a) - jnp.exp(b) + lam_init


def _rope_tables(pos):
    half = ROT_DIM // 2
    inv = ROPE_THETA ** (-jnp.arange(half, dtype=F32) * 2.0 / ROT_DIM)
    ang = pos.astype(F32)[:, None] * inv[None, :]
    cos, sin = jnp.cos(ang), jnp.sin(ang)
    t = pos.shape[0]
    rest = HEAD_DIM - ROT_DIM
    zh = jnp.zeros((t, half), F32)
    c = jnp.concatenate([cos, cos, jnp.ones((t, rest), F32)], axis=-1)
    sn = jnp.concatenate([-sin, zh, jnp.zeros((t, rest), F32)], axis=-1)
    sp = jnp.concatenate([zh, sin, jnp.zeros((t, rest), F32)], axis=-1)
    rep = LANES // HEAD_DIM
    return tuple(jnp.tile(a, (1, rep)) for a in (c, sn, sp))


def _proj_kernel(x_ref, g_ref, w_ref, cos_ref, sn_ref, sp_ref, *out_refs, splits):
    xn = _rms(x_ref[...], g_ref[...], NORM_EPS).astype(MXU_DTYPE)
    cos, sn, sp = cos_ref[...], sn_ref[...], sp_ref[...]
    half = ROT_DIM // 2
    for o_ref, (start, width, rotary) in zip(out_refs, splits):
        y = _dot(xn, w_ref[:, start:start + width])
        for c in range(width // LANES):
            yc = y[:, c * LANES:(c + 1) * LANES]
            if rotary:
                yc = (yc * cos + pltpu.roll(yc, LANES - half, axis=1) * sn
                      + pltpu.roll(yc, half, axis=1) * sp)
            o_ref[:, c * LANES:(c + 1) * LANES] = yc


def _proj_call(x, g, w, tables, splits, tm):
    n, d = x.shape
    nt = tables[0].shape[0]
    assert n % tm == 0 and nt % tm == 0
    ntb = nt // tm
    row = lambda i: (i, 0)
    fixed = lambda i: (0, 0)
    tab = lambda i: (i % ntb, 0)
    in_specs = [pl.BlockSpec((tm, d), row), pl.BlockSpec((1, d), fixed), pl.BlockSpec(w.shape, fixed)]
    in_specs += [pl.BlockSpec((tm, LANES), tab)] * 3
    return pl.pallas_call(
        functools.partial(_proj_kernel, splits=splits),
        grid=(n // tm,),
        in_specs=in_specs,
        out_specs=[pl.BlockSpec((tm, wd), row) for _, wd, _ in splits],
        out_shape=[jax.ShapeDtypeStruct((n, wd), F32) for _, wd, _ in splits],
        compiler_params=_cparams(("arbitrary",), VMEM_LIMIT),
        name="norm_proj_rope",
    )(x, g.reshape(1, d), w, *tables)


def _ffn_kernel(x_ref, o_ref, wout_ref, g_ref, pa_ref, pb_ref, wa_ref, wb_ref, cwa_ref, cwb_ref,
                cba_ref, cbb_ref, wd_ref, gf_ref, xo_ref, sa_ref, sb_ref,
                x1_s, hn_s, acc_s, ua_s, ub_s, ha_s, hb_s, *, tiles_per_seq, halo, shift, final_norm):
    i, j = pl.program_id(0), pl.program_id(1)
    nj = pl.num_programs(1)
    tm = x_ref.shape[0]

    @pl.when(j == 0)
    def _():
        x1 = x_ref[...] + _dot(o_ref[...].astype(MXU_DTYPE), wout_ref[...])
        x1_s[...] = x1
        hn_s[...] = _rms(x1, g_ref[...], NORM_EPS).astype(MXU_DTYPE)
        acc_s[...] = jnp.zeros_like(acc_s)

    first = (i % tiles_per_seq) == 0

    @pl.when(first)
    def _():
        ua_s[0:halo] = pa_ref[...]
        ub_s[0:halo] = pb_ref[...]

    @pl.when(jnp.logical_not(first))
    def _():
        ua_s[0:halo] = ha_s[j]
        ub_s[0:halo] = hb_s[j]

    hn = hn_s[...]
    ua_s[halo:halo + tm] = _dot(hn, wa_ref[...])
    ub_s[halo:halo + tm] = _dot(hn, wb_ref[...])

    def conv(u_s, cw_ref, cb_ref):
        c = cb_ref[...]
        for tap in range(CONV_W):
            off = halo - (CONV_W - 1 - tap) * shift
            c = c + u_s[off:off + tm] * cw_ref[tap:tap + 1]
        return c

    a = conv(ua_s, cwa_ref, cba_ref)
    b = conv(ub_s, cwb_ref, cbb_ref)
    act = (_gelu(a) * b).astype(MXU_DTYPE)
    acc_s[...] += _dot(act, wd_ref[...])

    tail_a = ua_s[tm:tm + halo]
    tail_b = ub_s[tm:tm + halo]
    ha_s[j] = tail_a
    hb_s[j] = tail_b
    sa_ref[...] = tail_a
    sb_ref[...] = tail_b

    @pl.when(j == nj - 1)
    def _():
        xo = x1_s[...] + acc_s[...]
        if final_norm:
            xo = _rms(xo, gf_ref[...], NORM_EPS)
        xo_ref[...] = xo


def _ffn_call(x, o, w_out, g, prev, w_up, conv_w, conv_b, w_down, g_final, *, seq_rows, halo, shift,
              final_norm, tm, tf):
    n, d = x.shape
    dff = w_down.shape[0]
    assert seq_rows % tm == 0 and dff % tf == 0 and tm >= halo
    tiles_per_seq = seq_rows // tm
    nseq = n // seq_rows
    nj = dff // tf
    row = lambda i, j: (i, 0)
    fixed = lambda i, j: (0, 0)
    in_specs = [
        pl.BlockSpec((tm, d), row), pl.BlockSpec((tm, d), row),
        pl.BlockSpec(w_out.shape, fixed), pl.BlockSpec((1, d), fixed),
        pl.BlockSpec((None, halo, tf), lambda i, j: (i // tiles_per_seq, 0, j)),
        pl.BlockSpec((None, halo, tf), lambda i, j: (i // tiles_per_seq, 0, nj + j)),
        pl.BlockSpec((d, tf), lambda i, j: (0, j)), pl.BlockSpec((d, tf), lambda i, j: (0, nj + j)),
        pl.BlockSpec((CONV_W, tf), lambda i, j: (0, j)), pl.BlockSpec((CONV_W, tf), lambda i, j: (0, nj + j)),
        pl.BlockSpec((1, tf), lambda i, j: (0, j)), pl.BlockSpec((1, tf), lambda i, j: (0, nj + j)),
        pl.BlockSpec((tf, d), lambda i, j: (j, 0)), pl.BlockSpec((1, d), fixed),
    ]
    tail_spec = pl.BlockSpec((None, halo, tf), lambda i, j: (i, 0, j))
    xo, ta, tb = pl.pallas_call(
        functools.partial(_ffn_kernel, tiles_per_seq=tiles_per_seq, halo=halo, shift=shift,
                          final_norm=final_norm),
        grid=(n // tm, nj),
        in_specs=in_specs,
        out_specs=[pl.BlockSpec((tm, d), row), tail_spec, tail_spec],
        out_shape=[jax.ShapeDtypeStruct((n, d), F32), jax.ShapeDtypeStruct((n // tm, halo, dff), F32),
                   jax.ShapeDtypeStruct((n // tm, halo, dff), F32)],
        scratch_shapes=[
            pltpu.VMEM((tm, d), F32), pltpu.VMEM((tm, d), MXU_DTYPE), pltpu.VMEM((tm, d), F32),
            pltpu.VMEM((halo + tm, tf), F32), pltpu.VMEM((halo + tm, tf), F32),
            pltpu.VMEM((nj, halo, tf), F32), pltpu.VMEM((nj, halo, tf), F32),
        ],
        compiler_params=_cparams(("arbitrary", "arbitrary"), VMEM_LIMIT),
        name="outproj_convffn",
    )(x, o, w_out, g.reshape(1, d), prev, prev, w_up, w_up, conv_w, conv_w,
      conv_b.reshape(1, -1), conv_b.reshape(1, -1), w_down, g_final.reshape(1, d))
    last = slice(tiles_per_seq - 1, None, tiles_per_seq)
    return xo, jnp.concatenate([ta[last], tb[last]], axis=-1)


def _diff_finish(acc, l, lam, g, lam_init, rows):
    o = acc[0:rows] / l[0:rows] - lam * (acc[rows:2 * rows] / l[rows:2 * rows])
    return _rms(o, g, SUBLN_EPS) * (1.0 - lam_init)


def _diff_prompt_kernel(q_ref, k_ref, v_ref, lam_ref, g_ref, o_ref, qs_s, m_s, l_s, acc_s, *, lam_init):
    qi = pl.program_id(2)
    tq = q_ref.shape[0]
    rows = A_GROUP * tq
    q = q_ref[...]
    lane = _lane_iota((tq, LANES))
    for m in range(2):
        for g in range(A_GROUP):
            chunk = q[:, g * LANES:(g + 1) * LANES]
            keep = (lane >= HALF) if m else (lane < HALF)
            r0 = (m * A_GROUP + g) * tq
            qs_s[r0:r0 + tq] = (jnp.where(keep, chunk, 0.0) * SCALE).astype(MXU_DTYPE)
    m_s[...] = jnp.full_like(m_s, NEG)
    l_s[...] = jnp.zeros_like(l_s)
    acc_s[...] = jnp.zeros_like(acc_s)

    def step(j, masked):
        start = pl.multiple_of(j * tq, tq)
        k = k_ref[pl.ds(start, tq), :].astype(MXU_DTYPE)
        v = v_ref[pl.ds(start, tq), :].astype(MXU_DTYPE)
        s = _dot_nt(qs_s[...], k)
        if masked:
            r = _row_iota(s.shape) & (tq - 1)
            s = jnp.where(_lane_iota(s.shape) <= r, s, NEG)
        m_prev = m_s[...]
        m_new = jnp.maximum(m_prev, jnp.max(s, axis=-1, keepdims=True))
        alpha = jnp.exp(m_prev - m_new)
        p = jnp.exp(s - m_new)
        l_s[...] = alpha * l_s[...] + jnp.sum(p, axis=-1, keepdims=True)
        acc_s[...] = alpha * acc_s[...] + _dot(p.astype(MXU_DTYPE), v)
        m_s[...] = m_new

    step(qi, True)
    lax.fori_loop(0, qi, lambda j, c: (step(j, False), c)[1], 0)

    lam = _diff_lambda(lam_ref[...], lam_init)
    o = _diff_finish(acc_s[...], l_s[...], lam, g_ref[...], lam_init, rows)
    for g in range(A_GROUP):
        o_ref[:, g * LANES:(g + 1) * LANES] = o[g * tq:(g + 1) * tq]


def _diff_prompt_call(qa, ka, va, lam_p, subln_g, lam_init, tq):
    b, s, _ = qa.shape
    assert s % tq == 0 and (tq & (tq - 1)) == 0
    kvw = 2 * HEAD_DIM
    qw = A_GROUP * kvw
    rows = 2 * A_GROUP * tq
    return pl.pallas_call(
        functools.partial(_diff_prompt_kernel, lam_init=lam_init),
        grid=(b, A_KV_HEADS, s // tq),
        in_specs=[
            pl.BlockSpec((None, tq, qw), lambda bi, h, i: (bi, i, h)),
            pl.BlockSpec((None, s, kvw), lambda bi, h, i: (bi, 0, h)),
            pl.BlockSpec((None, s, kvw), lambda bi, h, i: (bi, 0, h)),
            pl.BlockSpec(lam_p.shape, lambda bi, h, i: (0, 0)),
            pl.BlockSpec((1, kvw), lambda bi, h, i: (0, 0)),
        ],
        out_specs=pl.BlockSpec((None, tq, qw), lambda bi, h, i: (bi, i, h)),
        out_shape=jax.ShapeDtypeStruct((b, s, A_KV_HEADS * qw), F32),
        scratch_shapes=[pltpu.VMEM((rows, LANES), MXU_DTYPE), pltpu.VMEM((rows, 1), F32),
                        pltpu.VMEM((rows, 1), F32), pltpu.VMEM((rows, LANES), F32)],
        compiler_params=_cparams(("arbitrary", "arbitrary", "arbitrary"), VMEM_LIMIT),
        name="diff_attn_prompt",
    )(qa, ka, va, lam_p, subln_g.reshape(1, kvw))


def _topk_select(gate, valid, n_cmp, periodic):
    gm = jnp.where(valid, gate, -jnp.inf)
    lane = _lane_iota(gm.shape)
    idx = (lane & (n_cmp - 1)) if periodic else lane
    rank = jnp.zeros(gm.shape, F32)
    for k in range(1, n_cmp):
        lower = pltpu.roll(gm, k, axis=1)
        if periodic:
            beats = (lower > gm) | ((lower == gm) & (idx >= k))
            rank = rank + beats.astype(F32)
        else:
            rank = rank + (lower >= gm).astype(F32)
            higher = pltpu.roll(gm, LANES - k, axis=1)
            rank = rank + (higher > gm).astype(F32)
    return valid & (rank < MOBA_TOPK)


def _moba_prompt_kernel(q_ref, k_ref, v_ref, o_ref, qs_s, qf_s, km_s, m_s, l_s, acc_s, *, nb):
    qi = pl.program_id(2)
    tq = q_ref.shape[0]
    q = q_ref[...]
    for kvp in range(2):
        for g in range(B_GROUP):
            piece = _half_place(q[:, kvp * LANES:(kvp + 1) * LANES], g, kvp)
            r0 = (kvp * B_GROUP + g) * tq
            qf_s[r0:r0 + tq] = piece
            qs_s[r0:r0 + tq] = (piece * SCALE).astype(MXU_DTYPE)

    @pl.when(qi == 0)
    def _():
        for n in range(nb):
            row = jnp.mean(k_ref[n * MOBA_BLOCK:(n + 1) * MOBA_BLOCK, :], axis=0, keepdims=True)
            for rep in range(LANES // nb):
                km_s[rep * nb + n:rep * nb + n + 1, :] = row

    m_s[...] = jnp.full_like(m_s, NEG)
    l_s[...] = jnp.zeros_like(l_s)
    acc_s[...] = jnp.zeros_like(acc_s)

    gate = lax.dot_general(qf_s[...], km_s[...], (((1,), (1,)), ((), ())),
                           precision=lax.Precision.HIGHEST, preferred_element_type=F32)
    lane = _lane_iota(gate.shape)
    sel = _topk_select(gate, (lane & (nb - 1)) < qi, nb, True).astype(F32)

    def step(j, diagonal):
        start = pl.multiple_of(j * tq, tq)
        k = k_ref[pl.ds(start, tq), :].astype(MXU_DTYPE)
        v = v_ref[pl.ds(start, tq), :].astype(MXU_DTYPE)
        s = _dot_nt(qs_s[...], k)
        if diagonal:
            r = _row_iota(s.shape) & (tq - 1)
            s = jnp.where(_lane_iota(s.shape) <= r, s, NEG)
        else:
            picked = jnp.sum(jnp.where(lane == j, sel, 0.0), axis=-1, keepdims=True)
            s = jnp.where(picked > 0.0, s, NEG)
        m_prev = m_s[...]
        m_new = jnp.maximum(m_prev, jnp.max(s, axis=-1, keepdims=True))
        alpha = jnp.exp(m_prev - m_new)
        p = jnp.exp(s - m_new)
        l_s[...] = alpha * l_s[...] + jnp.sum(p, axis=-1, keepdims=True)
        acc_s[...] = alpha * acc_s[...] + _dot(p.astype(MXU_DTYPE), v)
        m_s[...] = m_new

    step(qi, True)
    lax.fori_loop(0, qi, lambda j, c: (step(j, False), c)[1], 0)

    o = acc_s[...] / l_s[...]
    for kvp in range(2):
        r0 = kvp * B_GROUP * tq
        o_ref[:, kvp * LANES:(kvp + 1) * LANES] = _merge_halves(o[r0:r0 + tq], kvp, o[r0 + tq:r0 + 2 * tq], kvp)


def _moba_prompt_call(qb, kb, vb):
    b, s, _ = qb.shape
    tq = MOBA_BLOCK
    nb = s // tq
    assert s % tq == 0 and LANES % nb == 0 and B_GROUP == 2
    pairs = B_KV_HEADS // 2
    qw = 2 * B_GROUP * HEAD_DIM
    rows = 2 * B_GROUP * tq
    return pl.pallas_call(
        functools.partial(_moba_prompt_kernel, nb=nb),
        grid=(b, pairs, nb),
        in_specs=[
            pl.BlockSpec((None, tq, qw), lambda bi, h, i: (bi, i, h)),
            pl.BlockSpec((None, s, LANES), lambda bi, h, i: (bi, 0, h)),
            pl.BlockSpec((None, s, LANES), lambda bi, h, i: (bi, 0, h)),
        ],
        out_specs=pl.BlockSpec((None, tq, qw), lambda bi, h, i: (bi, i, h)),
        out_shape=jax.ShapeDtypeStruct((b, s, pairs * qw), F32),
        scratch_shapes=[pltpu.VMEM((rows, LANES), MXU_DTYPE), pltpu.VMEM((rows, LANES), F32),
                        pltpu.VMEM((LANES, LANES), F32), pltpu.VMEM((rows, 1), F32),
                        pltpu.VMEM((rows, 1), F32), pltpu.VMEM((rows, LANES), F32)],
        compiler_params=_cparams(("arbitrary", "arbitrary", "arbitrary"), VMEM_LIMIT),
        name="moba_attn_prompt",
    )(qb, kb, vb)


def _window_q(q, rows):
    pieces = []
    for h in range(C_HEADS):
        chunk = q[:, (h // 2) * LANES:(h // 2 + 1) * LANES]
        pieces.append(_half_place(chunk, h % 2, h // C_GROUP))
    return (jnp.concatenate(pieces, axis=0) * SCALE).astype(MXU_DTYPE)


def _sink_column(sink_ref, rows):
    return jnp.concatenate([jnp.full((rows, 1), sink_ref[h], F32) for h in range(C_HEADS)], axis=0)


def _window_out(o, rows):
    chunks = []
    for c in range(C_HEADS // 2):
        kv = (2 * c) // C_GROUP
        lo = o[(2 * c) * rows:(2 * c + 1) * rows]
        hi = o[(2 * c + 1) * rows:(2 * c + 2) * rows]
        chunks.append(_merge_halves(lo, kv, hi, kv))
    return chunks


def _sink_softmax_pv(s, sink, v):
    m = jnp.maximum(jnp.max(s, axis=-1, keepdims=True), sink)
    p = jnp.exp(s - m)
    den = jnp.sum(p, axis=-1, keepdims=True) + jnp.exp(sink - m)
    return _dot(p.astype(MXU_DTYPE), v) / den


def _local_prompt_kernel(sink_ref, q_ref, k_ref, v_ref, o_ref, *, blocks):
    i = pl.program_id(1)
    sink = _sink_column(sink_ref, WINDOW)
    for blk in range(blocks):
        q0 = (i * blocks + blk) * WINDOW
        k0 = pl.multiple_of(jnp.maximum(q0 - WINDOW, 0), WINDOW)
        k = k_ref[pl.ds(k0, 2 * WINDOW), :].astype(MXU_DTYPE)
        v = v_ref[pl.ds(k0, 2 * WINDOW), :].astype(MXU_DTYPE)
        qs = _window_q(q_ref[blk * WINDOW:(blk + 1) * WINDOW, :], WINDOW)
        s = _dot_nt(qs, k)
        rel = (q0 + (_row_iota(s.shape) & (WINDOW - 1))) - (k0 + _lane_iota(s.shape))
        s = jnp.where((rel >= 0) & (rel < WINDOW), s, NEG)
        o = _sink_softmax_pv(s, sink, v)
        for c, chunk in enumerate(_window_out(o, WINDOW)):
            o_ref[blk * WINDOW:(blk + 1) * WINDOW, c * LANES:(c + 1) * LANES] = chunk


def _local_prompt_call(q, k, v, sinks, blocks):
    b, s, qw = q.shape
    tq = blocks * WINDOW
    assert s % tq == 0 and s >= 2 * WINDOW and C_KV_HEADS * HEAD_DIM == LANES
    return pl.pallas_call(
        functools.partial(_local_prompt_kernel, blocks=blocks),
        grid=(b, s // tq),
        in_specs=[
            pl.BlockSpec(memory_space=pltpu.SMEM),
            pl.BlockSpec((None, tq, qw), lambda bi, i: (bi, i, 0)),
            pl.BlockSpec((None, s, LANES), lambda bi, i: (bi, 0, 0)),
            pl.BlockSpec((None, s, LANES), lambda bi, i: (bi, 0, 0)),
        ],
        out_specs=pl.BlockSpec((None, tq, qw), lambda bi, i: (bi, i, 0)),
        out_shape=jax.ShapeDtypeStruct((b, s, qw), F32),
        compiler_params=_cparams(("arbitrary", "arbitrary"), VMEM_LIMIT),
        name="window_attn_prompt",
    )(sinks, q, k, v)


def _local_decode_kernel(sink_ref, q_ref, ck_ref, cv_ref, kn_ref, vn_ref, o_ref, ko_ref, vo_ref):
    nseq, t, _ = q_ref.shape
    buf = ck_ref.shape[1]
    sink = _sink_column(sink_ref, t)
    pad = jnp.zeros((2 * WINDOW - buf - t, LANES), F32)
    for sq in range(nseq):
        kn, vn = kn_ref[sq], vn_ref[sq]
        k = jnp.concatenate([ck_ref[sq], kn, pad], axis=0).astype(MXU_DTYPE)
        v = jnp.concatenate([cv_ref[sq], vn, pad], axis=0).astype(MXU_DTYPE)
        s = _dot_nt(_window_q(q_ref[sq], t), k)
        col = _lane_iota(s.shape)
        rel = buf + (_row_iota(s.shape) & (t - 1)) - col
        s = jnp.where((rel >= 0) & (rel < WINDOW) & (col < buf + t), s, NEG)
        o = _sink_softmax_pv(s, sink, v)
        for c, chunk in enumerate(_window_out(o, t)):
            o_ref[sq, :, c * LANES:(c + 1) * LANES] = chunk
        ko_ref[sq, 0:buf - t] = ck_ref[sq, t:buf]
        ko_ref[sq, buf - t:buf] = kn
        vo_ref[sq, 0:buf - t] = cv_ref[sq, t:buf]
        vo_ref[sq, buf - t:buf] = vn


def _local_decode_call(q, ck, cv, kn, vn, sinks, nseq):
    n, t, qw = q.shape
    buf = ck.shape[1]
    assert n % nseq == 0 and buf + t <= 2 * WINDOW and (t & (t - 1)) == 0 and t % SUBLANES == 0
    blk = lambda shape: pl.BlockSpec((nseq,) + shape, lambda i: (i, 0, 0))
    return pl.pallas_call(
        _local_decode_kernel,
        grid=(n // nseq,),
        in_specs=[pl.BlockSpec(memory_space=pltpu.SMEM), blk((t, qw)), blk((buf, LANES)), blk((buf, LANES)),
                  blk((t, LANES)), blk((t, LANES))],
        out_specs=[blk((t, qw)), blk((buf, LANES)), blk((buf, LANES))],
        out_shape=[jax.ShapeDtypeStruct((n, t, qw), F32), jax.ShapeDtypeStruct((n, buf, LANES), F32),
                   jax.ShapeDtypeStruct((n, buf, LANES), F32)],
        compiler_params=_cparams(("arbitrary",), VMEM_LIMIT),
        name="window_attn_decode",
    )(sinks, q, ck, cv, kn, vn)


def _diff_decode_kernel(pt_ref, q_ref, kn_ref, vn_ref, lam_ref, g_ref, *refs, pp, lam_init):
    k_refs, v_refs = refs[:pp], refs[pp:2 * pp]
    o_ref = refs[2 * pp]
    qs_s, m_s, l_s, acc_s = refs[2 * pp + 1:]
    gi = pl.program_id(1)
    t = q_ref.shape[0]
    page = k_refs[0].shape[0]
    kw = A_KV_HEADS * 2 * HEAD_DIM
    rows = A_KV_HEADS * 2 * A_GROUP * t

    @pl.when(gi == 0)
    def _():
        q = q_ref[...]
        lane = _lane_iota((t, LANES))
        zero = jnp.zeros((t, LANES), F32)
        for kv in range(A_KV_HEADS):
            for m in range(2):
                for g in range(A_GROUP):
                    c = kv * A_GROUP + g
                    keep = (lane >= HALF) if m else (lane < HALF)
                    piece = jnp.where(keep, q[:, c * LANES:(c + 1) * LANES], 0.0) * SCALE
                    full = jnp.concatenate([piece if kk == kv else zero for kk in range(A_KV_HEADS)], axis=1)
                    r0 = ((kv * 2 + m) * A_GROUP + g) * t
                    qs_s[r0:r0 + t] = full
        m_s[...] = jnp.full_like(m_s, NEG)
        l_s[...] = jnp.zeros_like(l_s)
        acc_s[...] = jnp.zeros_like(acc_s)

    def update(s, vs):
        m_prev = m_s[...]
        m_new = jnp.maximum(m_prev, jnp.max(s, axis=-1, keepdims=True))
        alpha = jnp.exp(m_prev - m_new)
        p32 = jnp.exp(s - m_new)
        p = p32.astype(MXU_DTYPE)
        pv = _dot(p[:, 0:page], vs[0])
        for i in range(1, len(vs)):
            pv = pv + _dot(p[:, i * page:(i + 1) * page], vs[i])
        l_s[...] = alpha * l_s[...] + jnp.sum(p32, axis=-1, keepdims=True)
        acc_s[...] = alpha * acc_s[...] + pv
        m_s[...] = m_new

    qs = qs_s[...].astype(MXU_DTYPE)
    s = jnp.concatenate([_dot_nt(qs, kr[...].astype(MXU_DTYPE)) for kr in k_refs], axis=1)
    update(s, [vr[...].astype(MXU_DTYPE) for vr in v_refs])

    @pl.when(gi == pl.num_programs(1) - 1)
    def _():
        pad = jnp.zeros((page - t, kw), F32)
        kn = jnp.concatenate([kn_ref[...], pad], axis=0).astype(MXU_DTYPE)
        vn = jnp.concatenate([vn_ref[...], pad], axis=0).astype(MXU_DTYPE)
        sn = _dot_nt(qs, kn)
        sn = jnp.where(_lane_iota(sn.shape) <= (_row_iota(sn.shape) & (t - 1)), sn, NEG)
        update(sn, [vn])
        lam = _diff_lambda(lam_ref[...], lam_init)
        acc, l = acc_s[...], l_s[...]
        gr = A_GROUP * t
        for kv in range(A_KV_HEADS):
            r0 = kv * 2 * gr
            o = _diff_finish(acc[r0:r0 + 2 * gr, kv * LANES:(kv + 1) * LANES], l[r0:r0 + 2 * gr],
                             lam, g_ref[...], lam_init, gr)
            for g in range(A_GROUP):
                c = kv * A_GROUP + g
                o_ref[:, c * LANES:(c + 1) * LANES] = o[g * t:(g + 1) * t]


def _paged_specs(layer, pp, n_pages, page, width):
    def make(i):
        return pl.BlockSpec((None, None, page, width),
                            lambda b, g, pt: (layer, pt[b * n_pages + g * pp + i], 0, 0))
    return [make(i) for i in range(pp)]


def _diff_decode_call(q, kn, vn, cache_k, cache_v, layer, pt_flat, n_pages, lam_p, subln_g, lam_init, pp):
    n, t, qw = q.shape
    page, kw = cache_k.shape[2], cache_k.shape[3]
    assert n_pages % pp == 0 and (t & (t - 1)) == 0 and t % SUBLANES == 0 and t <= page
    rows = A_KV_HEADS * 2 * A_GROUP * t
    per_seq = lambda w: pl.BlockSpec((None, t, w), lambda b, g, pt: (b, 0, 0))
    fixed = lambda shape: pl.BlockSpec(shape, lambda b, g, pt: (0, 0))
    grid_spec = pltpu.PrefetchScalarGridSpec(
        num_scalar_prefetch=1,
        grid=(n, n_pages // pp),
        in_specs=[per_seq(qw), per_seq(kw), per_seq(kw), fixed(lam_p.shape), fixed((1, LANES))]
        + _paged_specs(layer, pp, n_pages, page, kw) + _paged_specs(layer, pp, n_pages, page, kw),
        out_specs=per_seq(qw),
        scratch_shapes=[pltpu.VMEM((rows, kw), F32), pltpu.VMEM((rows, 1), F32),
                        pltpu.VMEM((rows, 1), F32), pltpu.VMEM((rows, kw), F32)],
    )
    return pl.pallas_call(
        functools.partial(_diff_decode_kernel, pp=pp, lam_init=lam_init),
        grid_spec=grid_spec,
        out_shape=jax.ShapeDtypeStruct((n, t, qw), F32),
        compiler_params=_cparams(("arbitrary", "arbitrary"), VMEM_LIMIT),
        name="diff_attn_decode",
    )(pt_flat, q, kn, vn, lam_p, subln_g.reshape(1, LANES), *([cache_k] * pp), *([cache_v] * pp))


def _moba_decode_kernel(pt_ref, q_ref, kn_ref, vn_ref, *refs, pp, n_blocks):
    k_refs, v_refs = refs[:pp], refs[pp:2 * pp]
    o_ref = refs[2 * pp]
    qf_s, m_s, l_s, g_s, o_s = refs[2 * pp + 1:]
    gi = pl.program_id(1)
    t = q_ref.shape[0]
    page = k_refs[0].shape[0]
    per_block = MOBA_BLOCK // page
    kw = B_KV_HEADS * HEAD_DIM
    rows = B_HEADS * t

    @pl.when(gi == 0)
    def _():
        q = q_ref[...]
        zero = jnp.zeros((t, LANES), F32)
        for kv in range(B_KV_HEADS):
            for g in range(B_GROUP):
                piece = _half_place(q[:, kv * LANES:(kv + 1) * LANES], g, kv % 2)
                full = jnp.concatenate([piece if c == kv // 2 else zero for c in range(kw // LANES)], axis=1)
                r0 = (kv * B_GROUP + g) * t
                qf_s[r0:r0 + t] = full
        m_s[...] = jnp.full_like(m_s, NEG)
        l_s[...] = jnp.zeros_like(l_s)
        g_s[...] = jnp.zeros_like(g_s)

    qf = qf_s[...]
    qs = (qf * SCALE).astype(MXU_DTYPE)
    lane = _lane_iota((rows, LANES))

    for bi in range(pp // per_block):
        ks = [k_refs[bi * per_block + i][...] for i in range(per_block)]
        vs = [v_refs[bi * per_block + i][...].astype(MXU_DTYPE) for i in range(per_block)]
        jb = gi * (pp // per_block) + bi
        s = jnp.concatenate([_dot_nt(qs, k.astype(MXU_DTYPE)) for k in ks], axis=1)
        mb = jnp.max(s, axis=-1, keepdims=True)
        p32 = jnp.exp(s - mb)
        p = p32.astype(MXU_DTYPE)
        ob = _dot(p[:, 0:page], vs[0])
        for i in range(1, per_block):
            ob = ob + _dot(p[:, i * page:(i + 1) * page], vs[i])
        lb = jnp.sum(p32, axis=-1, keepdims=True)
        ksum = jnp.sum(ks[0], axis=0, keepdims=True)
        for i in range(1, per_block):
            ksum = ksum + jnp.sum(ks[i], axis=0, keepdims=True)
        gate = jnp.sum(qf * (ksum * (1.0 / MOBA_BLOCK)), axis=-1, keepdims=True)
        here = lane == jb
        m_s[...] = jnp.where(here, mb, m_s[...])
        l_s[...] = jnp.where(here, lb, l_s[...])
        g_s[...] = jnp.where(here, gate, g_s[...])
        o_s[jb] = ob

    @pl.when(gi == pl.num_programs(1) - 1)
    def _():
        pad = jnp.zeros((page - t, kw), F32)
        kn = jnp.concatenate([kn_ref[...], pad], axis=0).astype(MXU_DTYPE)
        vn = jnp.concatenate([vn_ref[...], pad], axis=0).astype(MXU_DTYPE)
        sn = _dot_nt(qs, kn)
        sn = jnp.where(_lane_iota(sn.shape) <= (_row_iota(sn.shape) & (t - 1)), sn, NEG)
        m_own = jnp.max(sn, axis=-1, keepdims=True)
        p_own = jnp.exp(sn - m_own)
        l_own = jnp.sum(p_own, axis=-1, keepdims=True)
        o_own = _dot(p_own.astype(MXU_DTYPE), vn)

        sel = _topk_select(g_s[...], lane < n_blocks, n_blocks, False)
        mall = m_s[...]
        mx = jnp.maximum(jnp.max(jnp.where(sel, mall, NEG), axis=-1, keepdims=True), m_own)
        w = jnp.where(sel, jnp.exp(mall - mx), 0.0)
        w_own = jnp.exp(m_own - mx)
        den = jnp.sum(w * l_s[...], axis=-1, keepdims=True) + w_own * l_own
        num = w_own * o_own
        for jb in range(n_blocks):
            num = num + w[:, jb:jb + 1] * o_s[jb]
        o = num / den
        for kv in range(B_KV_HEADS):
            r0 = kv * B_GROUP * t
            src = o[r0:r0 + B_GROUP * t, (kv // 2) * LANES:(kv // 2 + 1) * LANES]
            o_ref[:, kv * LANES:(kv + 1) * LANES] = _merge_halves(src[0:t], kv % 2, src[t:2 * t], kv % 2)


def _moba_decode_call(q, kn, vn, cache_k, cache_v, layer, pt_flat, n_pages, pp):
    n, t, qw = q.shape
    page, kw = cache_k.shape[2], cache_k.shape[3]
    per_block = MOBA_BLOCK // page
    past = n_pages * page
    n_blocks = past // MOBA_BLOCK
    assert MOBA_BLOCK % page == 0 and pp % per_block == 0 and n_pages % pp == 0 and past % MOBA_BLOCK == 0
    assert n_blocks < LANES // 2 and t <= page and (t & (t - 1)) == 0 and t % SUBLANES == 0 and B_GROUP == 2
    rows = B_HEADS * t
    per_seq = lambda w: pl.BlockSpec((None, t, w), lambda b, g, pt: (b, 0, 0))
    grid_spec = pltpu.PrefetchScalarGridSpec(
        num_scalar_prefetch=1,
        grid=(n, n_pages // pp),
        in_specs=[per_seq(qw), per_seq(kw), per_seq(kw)]
        + _paged_specs(layer, pp, n_pages, page, kw) + _paged_specs(layer, pp, n_pages, page, kw),
        out_specs=per_seq(qw),
        scratch_shapes=[pltpu.VMEM((rows, kw), F32),
                        pltpu.VMEM((rows, LANES), F32), pltpu.VMEM((rows, LANES), F32),
                        pltpu.VMEM((rows, LANES), F32), pltpu.VMEM((n_blocks, rows, kw), F32)],
    )
    return pl.pallas_call(
        functools.partial(_moba_decode_kernel, pp=pp, n_blocks=n_blocks),
        grid_spec=grid_spec,
        out_shape=jax.ShapeDtypeStruct((n, t, qw), F32),
        compiler_params=_cparams(("arbitrary", "arbitrary"), VMEM_LIMIT),
        name="moba_attn_decode",
    )(pt_flat, q, kn, vn, *([cache_k] * pp), *([cache_v] * pp))


G_SPLITS = (
    (0, A_HEADS * 2 * HEAD_DIM, True), (512, A_KV_HEADS * 2 * HEAD_DIM, True), (768, A_KV_HEADS * 2 * HEAD_DIM, False),
    (1024, B_HEADS * HEAD_DIM, True), (1536, B_KV_HEADS * HEAD_DIM, True), (1792, B_KV_HEADS * HEAD_DIM, False),
)
L_SPLITS = ((0, C_HEADS * HEAD_DIM, True), (1024, C_KV_HEADS * HEAD_DIM, True), (1152, C_KV_HEADS * HEAD_DIM, False))


def kernel(x_prompt, x_sample, cache_a_k, cache_a_v, cache_b_k, cache_b_v, state_c_k, state_c_v, state_ffn,
           page_table, norm_mix, w_in_g, w_out_g, diff_lambda, diff_subln, w_in_l, w_out_l, sinks, norm_ffn,
           w_up, conv_w, conv_b, w_down, norm_final):
    bsz, seq, d = x_prompt.shape
    nd, t_dec, _ = x_sample.shape
    depth = norm_mix.shape[0]
    ng, pool, page = cache_a_k.shape[:3]
    n_pages = page_table.shape[1]
    past = n_pages * page
    buf = state_c_k.shape[2]
    dff2 = w_up.shape[2]

    cast = lambda w: w.astype(MXU_DTYPE)
    w_in_g, w_out_g, w_in_l, w_out_l, w_up_c, w_down_c = map(cast, (w_in_g, w_out_g, w_in_l, w_out_l, w_up, w_down))

    kw_a = A_KV_HEADS * 2 * HEAD_DIM
    kw_b = B_KV_HEADS * HEAD_DIM
    cak = cache_a_k.reshape(ng, pool, page, kw_a)
    cav = cache_a_v.reshape(ng, pool, page, kw_a)
    cbk = cache_b_k.reshape(ng, pool, page, kw_b)
    cbv = cache_b_v.reshape(ng, pool, page, kw_b)
    pt_flat = page_table.reshape(-1)

    tab_p = _rope_tables(jnp.arange(seq, dtype=jnp.int32))
    pos_s = past + jnp.repeat(jnp.arange(t_dec, dtype=jnp.int32), nd)
    tab_s = _rope_tables(pos_s)

    n_s = nd * t_dec
    xp = x_prompt.reshape(bsz * seq, d)
    xs = jnp.swapaxes(x_sample, 0, 1).reshape(n_s, d)

    tm_p = _tile(seq, 512)
    tm_s = _tile(n_s, 512)
    tf = 256 if (dff2 // 2) % 256 == 0 else LANES
    halo_p = SUBLANES
    halo_s = (CONV_W - 1) * nd
    pp = 16 if n_pages % 16 == 0 else n_pages
    dec_blk = 8 if nd % 8 == 0 else nd
    win_blocks = 4 if (seq // WINDOW) % 4 == 0 else 1

    def to_seq_major(a):
        return jnp.swapaxes(a.reshape(t_dec, nd, -1), 0, 1)

    def to_pos_major(a):
        return jnp.swapaxes(a, 0, 1).reshape(n_s, -1)

    outs = {k: [] for k in ("akp", "avp", "bkp", "bvp", "ckp", "cvp", "fp", "aks", "avs", "bks", "bvs", "cks", "cvs", "fs")}
    op = os_ = None
    w_prev = None
    for l in range(depth):
        i = l // 2
        if l % 2 == 0:
            lam_init = 0.8 - 0.6 * math.exp(-0.3 * l)
            qa, ka, va, qb, kb, vb = _proj_call(xp, norm_mix[l], w_in_g[i], tab_p, G_SPLITS, tm_p)
            r3 = lambda a: a.reshape(bsz, seq, -1)
            oa = _diff_prompt_call(r3(qa), r3(ka), r3(va), diff_lambda[i], diff_subln[i], lam_init, MOBA_BLOCK)
            ob = _moba_prompt_call(r3(qb), r3(kb), r3(vb))
            op = jnp.concatenate([oa, ob], axis=-1).reshape(bsz * seq, -1)
            outs["akp"].append(ka.reshape(bsz, seq, A_KV_HEADS, 2, HEAD_DIM))
            outs["avp"].append(va.reshape(bsz, seq, A_KV_HEADS, 2 * HEAD_DIM))
            outs["bkp"].append(kb.reshape(bsz, seq, B_KV_HEADS, HEAD_DIM))
            outs["bvp"].append(vb.reshape(bsz, seq, B_KV_HEADS, HEAD_DIM))

            qa, ka, va, qb, kb, vb = map(to_seq_major, _proj_call(xs, norm_mix[l], w_in_g[i], tab_s, G_SPLITS, tm_s))
            oa = _diff_decode_call(qa, ka, va, cak, cav, i, pt_flat, n_pages, diff_lambda[i], diff_subln[i],
                                   lam_init, pp)
            ob = _moba_decode_call(qb, kb, vb, cbk, cbv, i, pt_flat, n_pages, pp)
            os_ = to_pos_major(jnp.concatenate([oa, ob], axis=-1))
            outs["aks"].append(ka.reshape(nd, t_dec, A_KV_HEADS, 2, HEAD_DIM))
            outs["avs"].append(va.reshape(nd, t_dec, A_KV_HEADS, 2 * HEAD_DIM))
            outs["bks"].append(kb.reshape(nd, t_dec, B_KV_HEADS, HEAD_DIM))
            outs["bvs"].append(vb.reshape(nd, t_dec, B_KV_HEADS, HEAD_DIM))
            w_o = w_out_g[i]
        else:
            q, k, v = _proj_call(xp, norm_mix[l], w_in_l[i], tab_p, L_SPLITS, tm_p)
            r3 = lambda a: a.reshape(bsz, seq, -1)
            op = _local_prompt_call(r3(q), r3(k), r3(v), sinks[i], win_blocks)
            op = op.reshape(bsz * seq, -1)
            outs["ckp"].append(r3(k)[:, seq - buf:].reshape(bsz, buf, C_KV_HEADS, HEAD_DIM))
            outs["cvp"].append(r3(v)[:, seq - buf:].reshape(bsz, buf, C_KV_HEADS, HEAD_DIM))

            q, k, v = map(to_seq_major, _proj_call(xs, norm_mix[l], w_in_l[i], tab_s, L_SPLITS, tm_s))
            o, ck, cv = _local_decode_call(q, state_c_k[i].reshape(nd, buf, LANES), state_c_v[i].reshape(nd, buf, LANES),
                                           k, v, sinks[i], dec_blk)
            os_ = to_pos_major(o)
            outs["cks"].append(ck.reshape(nd, buf, C_KV_HEADS, HEAD_DIM))
            outs["cvs"].append(cv.reshape(nd, buf, C_KV_HEADS, HEAD_DIM))
            w_o = w_out_l[i]

        final = l == depth - 1
        prev_p = jnp.zeros((bsz, halo_p, dff2), F32)
        xp, tail_p = _ffn_call(xp, op, w_o, norm_ffn[l], prev_p, w_up_c[l], conv_w[l], conv_b[l], w_down_c[l],
                               norm_final, seq_rows=seq, halo=halo_p, shift=1, final_norm=final, tm=tm_p, tf=tf)
        outs["fp"].append(tail_p[:, halo_p - (CONV_W - 1):])
        prev_s = jnp.swapaxes(state_ffn[l], 0, 1).reshape(1, halo_s, dff2)
        xs, tail_s = _ffn_call(xs, os_, w_o, norm_ffn[l], prev_s, w_up_c[l], conv_w[l], conv_b[l], w_down_c[l],
                               norm_final, seq_rows=n_s, halo=halo_s, shift=nd, final_norm=final, tm=tm_s, tf=tf)
        outs["fs"].append(jnp.swapaxes(tail_s.reshape(CONV_W - 1, nd, dff2), 0, 1))

    st = lambda key: jnp.stack(outs[key])
    y_p = xp.reshape(bsz, seq, d)
    y_s = jnp.swapaxes(xs.reshape(t_dec, nd, d), 0, 1)
    return (y_p, y_s, st("akp"), st("avp"), st("bkp"), st("bvp"), st("ckp"), st("cvp"), st("fp"),
            st("aks"), st("avs"), st("bks"), st("bvs"), st("cks"), st("cvs"), st("fs"))
```

```python
import functools
import math

import jax
import jax.numpy as jnp
from jax import lax
from jax.experimental import pallas as pl
from jax.experimental.pallas import tpu as pltpu

HEAD_DIM = 64
SCALE = HEAD_DIM ** -0.5
ROT_DIM = HEAD_DIM // 4
ROPE_THETA = 500000.0
NORM_EPS = 1e-6
SUBLN_EPS = 1e-5
A_HEADS, A_KV_HEADS = 4, 2
A_GROUP = A_HEADS // A_KV_HEADS
B_HEADS, B_KV_HEADS = 8, 4
B_GROUP = B_HEADS // B_KV_HEADS
MOBA_BLOCK = 256
MOBA_TOPK = 3
C_HEADS, C_KV_HEADS = 16, 2
C_GROUP = C_HEADS // C_KV_HEADS
WINDOW = 128
CONV_W = 3

LANES = 128
SUBLANES = 8
HALF = LANES // 2
NEG = -1e30
ROW_CHUNK = 64
MXU_DTYPE = jnp.bfloat16
VMEM_LIMIT = 52 * 1024 * 1024

F32 = jnp.float32


def _cparams(sem, vmem=None):
    return pltpu.CompilerParams(dimension_semantics=sem, vmem_limit_bytes=vmem)


def _tile(n, pref):
    t = min(n, pref)
    while n % t or t % SUBLANES:
        t -= 1
    return t


def _lane_iota(shape):
    return lax.broadcasted_iota(jnp.int32, shape, len(shape) - 1)


def _row_iota(shape):
    return lax.broadcasted_iota(jnp.int32, shape, len(shape) - 2)


def _half_place(chunk, src_half, dst_half):
    x = chunk if src_half == dst_half else pltpu.roll(chunk, HALF, axis=1)
    lane = _lane_iota(x.shape)
    keep = (lane >= HALF) if dst_half else (lane < HALF)
    return jnp.where(keep, x, 0.0)


def _merge_halves(lo, lo_src_half, hi, hi_src_half):
    a = lo if lo_src_half == 0 else pltpu.roll(lo, HALF, axis=1)
    b = hi if hi_src_half == 1 else pltpu.roll(hi, HALF, axis=1)
    return jnp.where(_lane_iota(a.shape) < HALF, a, b)


def _dot_nt(a, b):
    return lax.dot_general(a, b, (((1,), (1,)), ((), ())), preferred_element_type=F32)


def _dot(a, b):
    return jnp.dot(a, b, preferred_element_type=F32)


def _rms(x, g, eps):
    ms = jnp.mean(x * x, axis=-1, keepdims=True)
    return x * lax.rsqrt(ms + eps) * g


def _gelu(x):
    return 0.5 * x * (1.0 + lax.erf(x * math.sqrt(0.5)))


def _diff_lambda(lp, lam_init):
    a = jnp.sum(lp[0:1] * lp[1:2], axis=-1, keepdims=True)
    b = jnp.sum(lp[2:3] * lp[3:4], axis=-1, keepdims=True)
    return jnp.exp(a) - jnp.exp(b) + lam_init


def _rope_tables(pos):
    half = ROT_DIM // 2
    inv = ROPE_THETA ** (-jnp.arange(half, dtype=F32) * 2.0 / ROT_DIM)
    ang = pos.astype(F32)[:, None] * inv[None, :]
    cos, sin = jnp.cos(ang), jnp.sin(ang)
    t = pos.shape[0]
    rest = HEAD_DIM - ROT_DIM
    zh = jnp.zeros((t, half), F32)
    c = jnp.concatenate([cos, cos, jnp.ones((t, rest), F32)], axis=-1)
    sn = jnp.concatenate([-sin, zh, jnp.zeros((t, rest), F32)], axis=-1)
    sp = jnp.concatenate([zh, sin, jnp.zeros((t, rest), F32)], axis=-1)
    rep = LANES // HEAD_DIM
    return tuple(jnp.tile(a, (1, rep)) for a in (c, sn, sp))


def _proj_kernel(x_ref, g_ref, w_ref, cos_ref, sn_ref, sp_ref, *out_refs, splits):
    xn = _rms(x_ref[...], g_ref[...], NORM_EPS).astype(MXU_DTYPE)
    cos, sn, sp = cos_ref[...], sn_ref[...], sp_ref[...]
    half = ROT_DIM // 2
    for o_ref, (start, width, rotary) in zip(out_refs, splits):
        y = _dot(xn, w_ref[:, start:start + width])
        for c in range(width // LANES):
            yc = y[:, c * LANES:(c + 1) * LANES]
            if rotary:
                yc = (yc * cos + pltpu.roll(yc, LANES - half, axis=1) * sn
                      + pltpu.roll(yc, half, axis=1) * sp)
            o_ref[:, c * LANES:(c + 1) * LANES] = yc


def _proj_call(x, g, w, tables, splits, tm):
    n, d = x.shape
    nt = tables[0].shape[0]
    assert n % tm == 0 and nt % tm == 0
    ntb = nt // tm
    row = lambda i: (i, 0)
    fixed = lambda i: (0, 0)
    tab = lambda i: (i % ntb, 0)
    in_specs = [pl.BlockSpec((tm, d), row), pl.BlockSpec((1, d), fixed), pl.BlockSpec(w.shape, fixed)]
    in_specs += [pl.BlockSpec((tm, LANES), tab)] * 3
    return pl.pallas_call(
        functools.partial(_proj_kernel, splits=splits),
        grid=(n // tm,),
        in_specs=in_specs,
        out_specs=[pl.BlockSpec((tm, wd), row) for _, wd, _ in splits],
        out_shape=[jax.ShapeDtypeStruct((n, wd), F32) for _, wd, _ in splits],
        compiler_params=_cparams(("arbitrary",), VMEM_LIMIT),
        name="norm_proj_rope",
    )(x, g.reshape(1, d), w, *tables)


def _ffn_kernel(x_ref, o_ref, wout_ref, g_ref, pa_ref, pb_ref, wa_ref, wb_ref, cwa_ref, cwb_ref,
                cba_ref, cbb_ref, wd_ref, gf_ref, xo_ref, sa_ref, sb_ref,
                x1_s, hn_s, acc_s, ua_s, ub_s, ha_s, hb_s, *, tiles_per_seq, halo, shift, final_norm):
    i, j = pl.program_id(0), pl.program_id(1)
    nj = pl.num_programs(1)
    tm = x_ref.shape[0]

    @pl.when(j == 0)
    def _():
        x1 = x_ref[...] + _dot(o_ref[...].astype(MXU_DTYPE), wout_ref[...])
        x1_s[...] = x1
        hn_s[...] = _rms(x1, g_ref[...], NORM_EPS).astype(MXU_DTYPE)
        acc_s[...] = jnp.zeros_like(acc_s)

    first = (i % tiles_per_seq) == 0

    @pl.when(first)
    def _():
        ua_s[0:halo] = pa_ref[...]
        ub_s[0:halo] = pb_ref[...]

    @pl.when(jnp.logical_not(first))
    def _():
        ua_s[0:halo] = ha_s[j]
        ub_s[0:halo] = hb_s[j]

    hn = hn_s[...]
    ua_s[halo:halo + tm] = _dot(hn, wa_ref[...])
    ub_s[halo:halo + tm] = _dot(hn, wb_ref[...])

    def conv(u_s, cw_ref, cb_ref):
        c = cb_ref[...]
        for tap in range(CONV_W):
            off = halo - (CONV_W - 1 - tap) * shift
            c = c + u_s[off:off + tm] * cw_ref[tap:tap + 1]
        return c

    a = conv(ua_s, cwa_ref, cba_ref)
    b = conv(ub_s, cwb_ref, cbb_ref)
    act = (_gelu(a) * b).astype(MXU_DTYPE)
    acc_s[...] += _dot(act, wd_ref[...])

    tail_a = ua_s[tm:tm + halo]
    tail_b = ub_s[tm:tm + halo]
    ha_s[j] = tail_a
    hb_s[j] = tail_b
    sa_ref[...] = tail_a
    sb_ref[...] = tail_b

    @pl.when(j == nj - 1)
    def _():
        xo = x1_s[...] + acc_s[...]
        if final_norm:
            xo = _rms(xo, gf_ref[...], NORM_EPS)
        xo_ref[...] = xo


def _ffn_call(x, o, w_out, g, prev, w_up, conv_w, conv_b, w_down, g_final, *, seq_rows, halo, shift,
              final_norm, tm, tf):
    n, d = x.shape
    dff = w_down.shape[0]
    assert seq_rows % tm == 0 and dff % tf == 0 and tm >= halo
    tiles_per_seq = seq_rows // tm
    nseq = n // seq_rows
    nj = dff // tf
    row = lambda i, j: (i, 0)
    fixed = lambda i, j: (0, 0)
    in_specs = [
        pl.BlockSpec((tm, d), row), pl.BlockSpec((tm, d), row),
        pl.BlockSpec(w_out.shape, fixed), pl.BlockSpec((1, d), fixed),
        pl.BlockSpec((None, halo, tf), lambda i, j: (i // tiles_per_seq, 0, j)),
        pl.BlockSpec((None, halo, tf), lambda i, j: (i // tiles_per_seq, 0, nj + j)),
        pl.BlockSpec((d, tf), lambda i, j: (0, j)), pl.BlockSpec((d, tf), lambda i, j: (0, nj + j)),
        pl.BlockSpec((CONV_W, tf), lambda i, j: (0, j)), pl.BlockSpec((CONV_W, tf), lambda i, j: (0, nj + j)),
        pl.BlockSpec((1, tf), lambda i, j: (0, j)), pl.BlockSpec((1, tf), lambda i, j: (0, nj + j)),
        pl.BlockSpec((tf, d), lambda i, j: (j, 0)), pl.BlockSpec((1, d), fixed),
    ]
    tail_spec = pl.BlockSpec((None, halo, tf), lambda i, j: (i, 0, j))
    xo, ta, tb = pl.pallas_call(
        functools.partial(_ffn_kernel, tiles_per_seq=tiles_per_seq, halo=halo, shift=shift,
                          final_norm=final_norm),
        grid=(n // tm, nj),
        in_specs=in_specs,
        out_specs=[pl.BlockSpec((tm, d), row), tail_spec, tail_spec],
        out_shape=[jax.ShapeDtypeStruct((n, d), F32), jax.ShapeDtypeStruct((n // tm, halo, dff), F32),
                   jax.ShapeDtypeStruct((n // tm, halo, dff), F32)],
        scratch_shapes=[
            pltpu.VMEM((tm, d), F32), pltpu.VMEM((tm, d), MXU_DTYPE), pltpu.VMEM((tm, d), F32),
            pltpu.VMEM((halo + tm, tf), F32), pltpu.VMEM((halo + tm, tf), F32),
            pltpu.VMEM((nj, halo, tf), F32), pltpu.VMEM((nj, halo, tf), F32),
        ],
        compiler_params=_cparams(("arbitrary", "arbitrary"), VMEM_LIMIT),
        name="outproj_convffn",
    )(x, o, w_out, g.reshape(1, d), prev, prev, w_up, w_up, conv_w, conv_w,
      conv_b.reshape(1, -1), conv_b.reshape(1, -1), w_down, g_final.reshape(1, d))
    last = slice(tiles_per_seq - 1, None, tiles_per_seq)
    return xo, jnp.concatenate([ta[last], tb[last]], axis=-1)


def _flash_update(qs_s, kt, v, m_s, l_s, acc_s, mask_fn):
    tk = kt.shape[1]
    for r0 in range(0, qs_s.shape[0], ROW_CHUNK):
        rs = slice(r0, r0 + ROW_CHUNK)
        s = _dot(qs_s[rs], kt)
        if mask_fn is not None:
            s = mask_fn(r0, s)
        parts = [s[:, i * LANES:(i + 1) * LANES] for i in range(tk // LANES)]
        mx = jnp.max(functools.reduce(jnp.maximum, parts), axis=-1, keepdims=True)
        m_prev = m_s[rs]
        m_new = jnp.maximum(m_prev, mx)
        alpha = jnp.exp(m_prev - m_new)
        ps = [jnp.exp(p - m_new) for p in parts]
        l_s[rs] = alpha * l_s[rs] + jnp.sum(functools.reduce(jnp.add, ps), axis=-1, keepdims=True)
        acc_s[rs] = alpha * acc_s[rs] + _dot(jnp.concatenate(ps, axis=1).astype(MXU_DTYPE), v)
        m_s[rs] = m_new


def _causal_mask(tq):
    def mask(r0, s):
        r = (_row_iota(s.shape) + r0) & (tq - 1)
        return jnp.where(_lane_iota(s.shape) <= r, s, NEG)
    return mask


def _diff_finish(acc, l, lam, g, lam_init, rows):
    o = acc[0:rows] / l[0:rows] - lam * (acc[rows:2 * rows] / l[rows:2 * rows])
    return _rms(o, g, SUBLN_EPS) * (1.0 - lam_init)


def _diff_prompt_kernel(q_ref, kt_ref, v_ref, lam_ref, g_ref, o_ref, qs_s, m_s, l_s, acc_s, *, lam_init):
    qi = pl.program_id(2)
    tq = q_ref.shape[0]
    rows = A_GROUP * tq
    q = q_ref[...]
    lane = _lane_iota((tq, LANES))
    for m in range(2):
        for g in range(A_GROUP):
            chunk = q[:, g * LANES:(g + 1) * LANES]
            keep = (lane >= HALF) if m else (lane < HALF)
            r0 = (m * A_GROUP + g) * tq
            qs_s[r0:r0 + tq] = (jnp.where(keep, chunk, 0.0) * SCALE).astype(MXU_DTYPE)
    m_s[...] = jnp.full_like(m_s, NEG)
    l_s[...] = jnp.zeros_like(l_s)
    acc_s[...] = jnp.zeros_like(acc_s)

    def step(j, mask_fn):
        start = pl.multiple_of(j * tq, tq)
        kt = kt_ref[:, pl.ds(start, tq)].astype(MXU_DTYPE)
        v = v_ref[pl.ds(start, tq), :].astype(MXU_DTYPE)
        _flash_update(qs_s, kt, v, m_s, l_s, acc_s, mask_fn)

    step(qi, _causal_mask(tq))
    lax.fori_loop(0, qi, lambda j, c: (step(j, None), c)[1], 0)

    lam = _diff_lambda(lam_ref[...], lam_init)
    o = _diff_finish(acc_s[...], l_s[...], lam, g_ref[...], lam_init, rows)
    for g in range(A_GROUP):
        o_ref[:, g * LANES:(g + 1) * LANES] = o[g * tq:(g + 1) * tq]


def _diff_prompt_call(qa, kat, va, lam_p, subln_g, lam_init, tq):
    b, s, _ = qa.shape
    assert s % tq == 0 and (tq & (tq - 1)) == 0 and (2 * A_GROUP * tq) % ROW_CHUNK == 0
    kvw = 2 * HEAD_DIM
    qw = A_GROUP * kvw
    rows = 2 * A_GROUP * tq
    return pl.pallas_call(
        functools.partial(_diff_prompt_kernel, lam_init=lam_init),
        grid=(b, A_KV_HEADS, s // tq),
        in_specs=[
            pl.BlockSpec((None, tq, qw), lambda bi, h, i: (bi, i, h)),
            pl.BlockSpec((None, kvw, s), lambda bi, h, i: (bi, h, 0)),
            pl.BlockSpec((None, s, kvw), lambda bi, h, i: (bi, 0, h)),
            pl.BlockSpec(lam_p.shape, lambda bi, h, i: (0, 0)),
            pl.BlockSpec((1, kvw), lambda bi, h, i: (0, 0)),
        ],
        out_specs=pl.BlockSpec((None, tq, qw), lambda bi, h, i: (bi, i, h)),
        out_shape=jax.ShapeDtypeStruct((b, s, A_KV_HEADS * qw), F32),
        scratch_shapes=[pltpu.VMEM((rows, LANES), MXU_DTYPE), pltpu.VMEM((rows, LANES), F32),
                        pltpu.VMEM((rows, LANES), F32), pltpu.VMEM((rows, LANES), F32)],
        compiler_params=_cparams(("arbitrary", "arbitrary", "arbitrary"), VMEM_LIMIT),
        name="diff_attn_prompt",
    )(qa, kat, va, lam_p, subln_g.reshape(1, kvw))


def _topk_select(gate, valid, n_cmp, periodic):
    gm = jnp.where(valid, gate, -jnp.inf)
    lane = _lane_iota(gm.shape)
    idx = (lane & (n_cmp - 1)) if periodic else lane
    rank = jnp.zeros(gm.shape, F32)
    for k in range(1, n_cmp):
        lower = pltpu.roll(gm, k, axis=1)
        if periodic:
            beats = (lower > gm) | ((lower == gm) & (idx >= k))
            rank = rank + beats.astype(F32)
        else:
            rank = rank + (lower >= gm).astype(F32)
            higher = pltpu.roll(gm, LANES - k, axis=1)
            rank = rank + (higher > gm).astype(F32)
    return valid & (rank < MOBA_TOPK)


def _topk_select_rows(gate_t, valid, n):
    gm = jnp.where(valid, gate_t, -jnp.inf)
    idx = _row_iota(gm.shape)
    rank = jnp.zeros(gm.shape, F32)
    for i in range(n):
        gi = gm[i:i + 1, :]
        beats = (gi > gm) | ((gi == gm) & (idx > i))
        rank = rank + beats.astype(F32)
    return valid & (rank < MOBA_TOPK)


def _moba_prompt_kernel(q_ref, kt_ref, v_ref, o_ref, qs_s, qf_s, km_s, sel_s, m_s, l_s, acc_s, *, nb):
    qi = pl.program_id(2)
    tq = q_ref.shape[0]
    nbp = -(-nb // SUBLANES) * SUBLANES
    q = q_ref[...]
    for kvp in range(2):
        for g in range(B_GROUP):
            piece = _half_place(q[:, kvp * LANES:(kvp + 1) * LANES], g, kvp)
            r0 = (kvp * B_GROUP + g) * tq
            qf_s[r0:r0 + tq] = piece
            qs_s[r0:r0 + tq] = (piece * SCALE).astype(MXU_DTYPE)

    @pl.when(qi == 0)
    def _():
        col_block = _lane_iota((LANES, LANES)) & (nb - 1)
        km = jnp.zeros((LANES, LANES), F32)
        for n in range(nb):
            mean_n = jnp.sum(kt_ref[:, n * MOBA_BLOCK:(n + 1) * MOBA_BLOCK], axis=-1, keepdims=True) * (1.0 / MOBA_BLOCK)
            km = jnp.where(col_block == n, mean_n, km)
        km_s[...] = km

    m_s[...] = jnp.full_like(m_s, NEG)
    l_s[...] = jnp.zeros_like(l_s)
    acc_s[...] = jnp.zeros_like(acc_s)

    gate = jnp.dot(qf_s[...], km_s[...], precision=lax.Precision.HIGHEST, preferred_element_type=F32)
    gate_t = jnp.transpose(gate)[0:nbp]
    sel_t = _topk_select_rows(gate_t, _row_iota(gate_t.shape) < jnp.minimum(qi, nb), nb).astype(F32)
    sel_t = jnp.concatenate([sel_t, jnp.zeros((LANES - nbp, gate.shape[0]), F32)], axis=0)
    sel_s[...] = jnp.transpose(sel_t)

    def step(j, mask_fn):
        start = pl.multiple_of(j * tq, tq)
        kt = kt_ref[:, pl.ds(start, tq)].astype(MXU_DTYPE)
        v = v_ref[pl.ds(start, tq), :].astype(MXU_DTYPE)
        _flash_update(qs_s, kt, v, m_s, l_s, acc_s, mask_fn)

    def past_step(j, carry):
        def picked_mask(r0, s):
            sel = sel_s[r0:r0 + ROW_CHUNK]
            picked = jnp.sum(jnp.where(_lane_iota(sel.shape) == j, sel, 0.0), axis=-1, keepdims=True)
            return jnp.where(picked > 0.0, s, NEG)
        step(j, picked_mask)
        return carry

    step(qi, _causal_mask(tq))
    lax.fori_loop(0, qi, past_step, 0)

    o = acc_s[...] / l_s[...]
    for kvp in range(2):
        r0 = kvp * B_GROUP * tq
        o_ref[:, kvp * LANES:(kvp + 1) * LANES] = _merge_halves(o[r0:r0 + tq], kvp, o[r0 + tq:r0 + 2 * tq], kvp)


def _moba_prompt_call(qb, kbt, vb):
    b, s, _ = qb.shape
    tq = MOBA_BLOCK
    nb = s // tq
    assert s % tq == 0 and LANES % nb == 0 and B_GROUP == 2
    pairs = B_KV_HEADS // 2
    qw = 2 * B_GROUP * HEAD_DIM
    rows = 2 * B_GROUP * tq
    assert rows % ROW_CHUNK == 0
    return pl.pallas_call(
        functools.partial(_moba_prompt_kernel, nb=nb),
        grid=(b, pairs, nb),
        in_specs=[
            pl.BlockSpec((None, tq, qw), lambda bi, h, i: (bi, i, h)),
            pl.BlockSpec((None, LANES, s), lambda bi, h, i: (bi, h, 0)),
            pl.BlockSpec((None, s, LANES), lambda bi, h, i: (bi, 0, h)),
        ],
        out_specs=pl.BlockSpec((None, tq, qw), lambda bi, h, i: (bi, i, h)),
        out_shape=jax.ShapeDtypeStruct((b, s, pairs * qw), F32),
        scratch_shapes=[pltpu.VMEM((rows, LANES), MXU_DTYPE), pltpu.VMEM((rows, LANES), F32),
                        pltpu.VMEM((LANES, LANES), F32), pltpu.VMEM((rows, LANES), F32),
                        pltpu.VMEM((rows, LANES), F32), pltpu.VMEM((rows, LANES), F32),
                        pltpu.VMEM((rows, LANES), F32)],
        compiler_params=_cparams(("arbitrary", "arbitrary", "arbitrary"), VMEM_LIMIT),
        name="moba_attn_prompt",
    )(qb, kbt, vb)


def _window_q(q, rows):
    pieces = []
    for h in range(C_HEADS):
        chunk = q[:, (h // 2) * LANES:(h // 2 + 1) * LANES]
        pieces.append(_half_place(chunk, h % 2, h // C_GROUP))
    return (jnp.concatenate(pieces, axis=0) * SCALE).astype(MXU_DTYPE)


def _sink_column(sink_ref, rows):
    return jnp.concatenate([jnp.full((rows, 1), sink_ref[h], F32) for h in range(C_HEADS)], axis=0)


def _window_out(o, rows):
    chunks = []
    for c in range(C_HEADS // 2):
        kv = (2 * c) // C_GROUP
        lo = o[(2 * c) * rows:(2 * c + 1) * rows]
        hi = o[(2 * c + 1) * rows:(2 * c + 2) * rows]
        chunks.append(_merge_halves(lo, kv, hi, kv))
    return chunks


def _sink_softmax_pv(s, sink, v):
    m = jnp.maximum(jnp.max(s, axis=-1, keepdims=True), sink)
    p = jnp.exp(s - m)
    den = jnp.sum(p, axis=-1, keepdims=True) + jnp.exp(sink - m)
    return _dot(p.astype(MXU_DTYPE), v) / den


def _local_prompt_kernel(sink_ref, q_ref, k_ref, v_ref, o_ref, *, blocks):
    i = pl.program_id(1)
    sink = _sink_column(sink_ref, WINDOW)
    for blk in range(blocks):
        q0 = (i * blocks + blk) * WINDOW
        k0 = pl.multiple_of(jnp.maximum(q0 - WINDOW, 0), WINDOW)
        k = k_ref[pl.ds(k0, 2 * WINDOW), :].astype(MXU_DTYPE)
        v = v_ref[pl.ds(k0, 2 * WINDOW), :].astype(MXU_DTYPE)
        qs = _window_q(q_ref[blk * WINDOW:(blk + 1) * WINDOW, :], WINDOW)
        s = _dot_nt(qs, k)
        rel = (q0 + (_row_iota(s.shape) & (WINDOW - 1))) - (k0 + _lane_iota(s.shape))
        s = jnp.where((rel >= 0) & (rel < WINDOW), s, NEG)
        o = _sink_softmax_pv(s, sink, v)
        for c, chunk in enumerate(_window_out(o, WINDOW)):
            o_ref[blk * WINDOW:(blk + 1) * WINDOW, c * LANES:(c + 1) * LANES] = chunk


def _local_prompt_call(q, k, v, sinks, blocks):
    b, s, qw = q.shape
    tq = blocks * WINDOW
    assert s % tq == 0 and s >= 2 * WINDOW and C_KV_HEADS * HEAD_DIM == LANES
    return pl.pallas_call(
        functools.partial(_local_prompt_kernel, blocks=blocks),
        grid=(b, s // tq),
        in_specs=[
            pl.BlockSpec(memory_space=pltpu.SMEM),
            pl.BlockSpec((None, tq, qw), lambda bi, i: (bi, i, 0)),
            pl.BlockSpec((None, s, LANES), lambda bi, i: (bi, 0, 0)),
            pl.BlockSpec((None, s, LANES), lambda bi, i: (bi, 0, 0)),
        ],
        out_specs=pl.BlockSpec((None, tq, qw), lambda bi, i: (bi, i, 0)),
        out_shape=jax.ShapeDtypeStruct((b, s, qw), F32),
        compiler_params=_cparams(("arbitrary", "arbitrary"), VMEM_LIMIT),
        name="window_attn_prompt",
    )(sinks, q, k, v)


def _local_decode_kernel(sink_ref, q_ref, ck_ref, cv_ref, kn_ref, vn_ref, o_ref, ko_ref, vo_ref):
    nseq, t, _ = q_ref.shape
    buf = ck_ref.shape[1]
    sink = _sink_column(sink_ref, t)
    pad = jnp.zeros((2 * WINDOW - buf - t, LANES), F32)
    for sq in range(nseq):
        kn, vn = kn_ref[sq], vn_ref[sq]
        k = jnp.concatenate([ck_ref[sq], kn, pad], axis=0).astype(MXU_DTYPE)
        v = jnp.concatenate([cv_ref[sq], vn, pad], axis=0).astype(MXU_DTYPE)
        s = _dot_nt(_window_q(q_ref[sq], t), k)
        col = _lane_iota(s.shape)
        rel = buf + (_row_iota(s.shape) & (t - 1)) - col
        s = jnp.where((rel >= 0) & (rel < WINDOW) & (col < buf + t), s, NEG)
        o = _sink_softmax_pv(s, sink, v)
        for c, chunk in enumerate(_window_out(o, t)):
            o_ref[sq, :, c * LANES:(c + 1) * LANES] = chunk
        ko_ref[sq, 0:buf - t] = ck_ref[sq, t:buf]
        ko_ref[sq, buf - t:buf] = kn
        vo_ref[sq, 0:buf - t] = cv_ref[sq, t:buf]
        vo_ref[sq, buf - t:buf] = vn


def _local_decode_call(q, ck, cv, kn, vn, sinks, nseq):
    n, t, qw = q.shape
    buf = ck.shape[1]
    assert n % nseq == 0 and buf + t <= 2 * WINDOW and (t & (t - 1)) == 0 and t % SUBLANES == 0
    blk = lambda shape: pl.BlockSpec((nseq,) + shape, lambda i: (i, 0, 0))
    return pl.pallas_call(
        _local_decode_kernel,
        grid=(n // nseq,),
        in_specs=[pl.BlockSpec(memory_space=pltpu.SMEM), blk((t, qw)), blk((buf, LANES)), blk((buf, LANES)),
                  blk((t, LANES)), blk((t, LANES))],
        out_specs=[blk((t, qw)), blk((buf, LANES)), blk((buf, LANES))],
        out_shape=[jax.ShapeDtypeStruct((n, t, qw), F32), jax.ShapeDtypeStruct((n, buf, LANES), F32),
                   jax.ShapeDtypeStruct((n, buf, LANES), F32)],
        compiler_params=_cparams(("arbitrary",), VMEM_LIMIT),
        name="window_attn_decode",
    )(sinks, q, ck, cv, kn, vn)


def _diff_decode_kernel(pt_ref, q_ref, kn_ref, vn_ref, lam_ref, g_ref, *refs, pp, lam_init):
    kt_refs, v_refs = refs[:pp], refs[pp:2 * pp]
    o_ref = refs[2 * pp]
    qs_s, m_s, l_s, acc_s = refs[2 * pp + 1:]
    gi = pl.program_id(1)
    t = q_ref.shape[0]
    kw, page = kt_refs[0].shape
    gr = 2 * A_GROUP * t

    @pl.when(gi == 0)
    def _():
        q = q_ref[...]
        lane = _lane_iota((t, LANES))
        zero = jnp.zeros((t, LANES), F32)
        for kv in range(A_KV_HEADS):
            for m in range(2):
                for g in range(A_GROUP):
                    c = kv * A_GROUP + g
                    keep = (lane >= HALF) if m else (lane < HALF)
                    piece = jnp.where(keep, q[:, c * LANES:(c + 1) * LANES], 0.0) * SCALE
                    full = jnp.concatenate([piece if kk == kv else zero for kk in range(A_KV_HEADS)], axis=1)
                    r0 = ((kv * 2 + m) * A_GROUP + g) * t
                    qs_s[r0:r0 + t] = full
        m_s[...] = jnp.full_like(m_s, NEG)
        l_s[...] = jnp.zeros_like(l_s)
        acc_s[...] = jnp.zeros_like(acc_s)

    def update(s, values_of_head):
        parts = [s[:, i * LANES:(i + 1) * LANES] for i in range(s.shape[1] // LANES)]
        mx = jnp.max(functools.reduce(jnp.maximum, parts), axis=-1, keepdims=True)
        m_prev = m_s[...]
        m_new = jnp.maximum(m_prev, mx)
        alpha = jnp.exp(m_prev - m_new)
        ps = [jnp.exp(p - m_new) for p in parts]
        l_s[...] = alpha * l_s[...] + jnp.sum(functools.reduce(jnp.add, ps), axis=-1, keepdims=True)
        p = jnp.concatenate(ps, axis=1).astype(MXU_DTYPE)
        pv = jnp.concatenate([_dot(p[kv * gr:(kv + 1) * gr], values_of_head(kv)) for kv in range(A_KV_HEADS)], axis=0)
        acc_s[...] = alpha * acc_s[...] + pv
        m_s[...] = m_new

    def paged_values(kv):
        return jnp.concatenate([vr[pl.ds(kv, page, stride=A_KV_HEADS), :].astype(MXU_DTYPE) for vr in v_refs], axis=0)

    qs = qs_s[...].astype(MXU_DTYPE)
    kt = jnp.concatenate([kr[...].astype(MXU_DTYPE) for kr in kt_refs], axis=1)
    update(_dot(qs, kt), paged_values)

    @pl.when(gi == pl.num_programs(1) - 1)
    def _():
        pad = jnp.zeros((LANES - t, kw), F32)
        kn = jnp.concatenate([kn_ref[...], pad], axis=0).astype(MXU_DTYPE)
        vn = jnp.concatenate([vn_ref[...], pad], axis=0).astype(MXU_DTYPE)
        sn = _dot_nt(qs, kn)
        sn = jnp.where(_lane_iota(sn.shape) <= (_row_iota(sn.shape) & (t - 1)), sn, NEG)
        update(sn, lambda kv: vn[:, kv * LANES:(kv + 1) * LANES])
        lam = _diff_lambda(lam_ref[...], lam_init)
        acc, l = acc_s[...], l_s[...]
        for kv in range(A_KV_HEADS):
            o = _diff_finish(acc[kv * gr:(kv + 1) * gr], l[kv * gr:(kv + 1) * gr], lam, g_ref[...], lam_init,
                             A_GROUP * t)
            for g in range(A_GROUP):
                c = kv * A_GROUP + g
                o_ref[:, c * LANES:(c + 1) * LANES] = o[g * t:(g + 1) * t]


def _paged_specs(layer, pp, n_pages, shape):
    def make(i):
        return pl.BlockSpec((None, None) + shape, lambda b, g, pt: (layer, pt[b * n_pages + g * pp + i], 0, 0))
    return [make(i) for i in range(pp)]


def _diff_decode_call(q, kn, vn, cache_kt, cache_v, layer, pt_flat, n_pages, lam_p, subln_g, lam_init, pp):
    n, t, qw = q.shape
    kw, page = cache_kt.shape[2], cache_kt.shape[3]
    assert n_pages % pp == 0 and (t & (t - 1)) == 0 and t % SUBLANES == 0 and t <= LANES and page == LANES
    assert cache_v.shape[2:] == (page * A_KV_HEADS, 2 * HEAD_DIM)
    rows = A_KV_HEADS * 2 * A_GROUP * t
    per_seq = lambda w: pl.BlockSpec((None, t, w), lambda b, g, pt: (b, 0, 0))
    fixed = lambda shape: pl.BlockSpec(shape, lambda b, g, pt: (0, 0))
    grid_spec = pltpu.PrefetchScalarGridSpec(
        num_scalar_prefetch=1,
        grid=(n, n_pages // pp),
        in_specs=[per_seq(qw), per_seq(kw), per_seq(kw), fixed(lam_p.shape), fixed((1, LANES))]
        + _paged_specs(layer, pp, n_pages, (kw, page))
        + _paged_specs(layer, pp, n_pages, (page * A_KV_HEADS, 2 * HEAD_DIM)),
        out_specs=per_seq(qw),
        scratch_shapes=[pltpu.VMEM((rows, kw), F32), pltpu.VMEM((rows, LANES), F32),
                        pltpu.VMEM((rows, LANES), F32), pltpu.VMEM((rows, LANES), F32)],
    )
    return pl.pallas_call(
        functools.partial(_diff_decode_kernel, pp=pp, lam_init=lam_init),
        grid_spec=grid_spec,
        out_shape=jax.ShapeDtypeStruct((n, t, qw), F32),
        compiler_params=_cparams(("arbitrary", "arbitrary"), VMEM_LIMIT),
        name="diff_attn_decode",
    )(pt_flat, q, kn, vn, lam_p, subln_g.reshape(1, LANES), *([cache_kt] * pp), *([cache_v] * pp))


def _moba_decode_kernel(pt_ref, q_ref, kn_ref, vn_ref, *refs, pp, n_blocks):
    kt_refs, vt_refs = refs[:pp], refs[pp:2 * pp]
    o_ref = refs[2 * pp]
    qf_s, m_s, l_s, g_s, o_s = refs[2 * pp + 1:]
    gi = pl.program_id(1)
    t = q_ref.shape[0]
    kw, page = kt_refs[0].shape
    per_block = MOBA_BLOCK // page
    rows = B_HEADS * t

    @pl.when(gi == 0)
    def _():
        q = q_ref[...]
        zero = jnp.zeros((t, LANES), F32)
        for kv in range(B_KV_HEADS):
            for g in range(B_GROUP):
                piece = _half_place(q[:, kv * LANES:(kv + 1) * LANES], g, kv % 2)
                full = jnp.concatenate([piece if c == kv // 2 else zero for c in range(kw // LANES)], axis=1)
                r0 = (kv * B_GROUP + g) * t
                qf_s[r0:r0 + t] = full
        m_s[...] = jnp.full_like(m_s, NEG)
        l_s[...] = jnp.zeros_like(l_s)
        g_s[...] = jnp.zeros_like(g_s)

    qf = qf_s[...]
    qs = (qf * SCALE).astype(MXU_DTYPE)
    lane = _lane_iota((rows, LANES))

    for bi in range(pp // per_block):
        pages = range(bi * per_block, (bi + 1) * per_block)
        kt = jnp.concatenate([kt_refs[i][...].astype(MXU_DTYPE) for i in pages], axis=1)
        vt = jnp.concatenate([vt_refs[i][...].astype(MXU_DTYPE) for i in pages], axis=1)
        jb = gi * (pp // per_block) + bi
        s = _dot(qs, kt)
        parts = [s[:, i * LANES:(i + 1) * LANES] for i in range(MOBA_BLOCK // LANES)]
        mb = jnp.max(functools.reduce(jnp.maximum, parts), axis=-1, keepdims=True)
        ps = [jnp.exp(p - mb) for p in parts]
        lb = jnp.sum(functools.reduce(jnp.add, ps), axis=-1, keepdims=True)
        ob = _dot_nt(jnp.concatenate(ps, axis=1).astype(MXU_DTYPE), vt)
        gate = jnp.sum(functools.reduce(jnp.add, parts), axis=-1, keepdims=True) * (1.0 / (MOBA_BLOCK * SCALE))
        here = lane == jb
        m_s[...] = jnp.where(here, mb, m_s[...])
        l_s[...] = jnp.where(here, lb, l_s[...])
        g_s[...] = jnp.where(here, gate, g_s[...])
        o_s[jb] = ob

    @pl.when(gi == pl.num_programs(1) - 1)
    def _():
        pad = jnp.zeros((LANES - t, kw), F32)
        kn = jnp.concatenate([kn_ref[...], pad], axis=0).astype(MXU_DTYPE)
        vn = jnp.concatenate([vn_ref[...], pad], axis=0).astype(MXU_DTYPE)
        sn = _dot_nt(qs, kn)
        sn = jnp.where(_lane_iota(sn.shape) <= (_row_iota(sn.shape) & (t - 1)), sn, NEG)
        m_own = jnp.max(sn, axis=-1, keepdims=True)
        p_own = jnp.exp(sn - m_own)
        l_own = jnp.sum(p_own, axis=-1, keepdims=True)
        o_own = _dot(p_own.astype(MXU_DTYPE), vn)

        sel = _topk_select(g_s[...], lane < n_blocks, n_blocks, False)
        mall = m_s[...]
        mx = jnp.maximum(jnp.max(jnp.where(sel, mall, NEG), axis=-1, keepdims=True), m_own)
        w = jnp.where(sel, jnp.exp(mall - mx), 0.0)
        w_own = jnp.exp(m_own - mx)
        den = jnp.sum(w * l_s[...], axis=-1, keepdims=True) + w_own * l_own
        num = w_own * o_own
        for jb in range(n_blocks):
            num = num + w[:, jb:jb + 1] * o_s[jb]
        o = num / den
        for kv in range(B_KV_HEADS):
            r0 = kv * B_GROUP * t
            src = o[r0:r0 + B_GROUP * t, (kv // 2) * LANES:(kv // 2 + 1) * LANES]
            o_ref[:, kv * LANES:(kv + 1) * LANES] = _merge_halves(src[0:t], kv % 2, src[t:2 * t], kv % 2)


def _moba_decode_call(q, kn, vn, cache_kt, cache_vt, layer, pt_flat, n_pages, pp):
    n, t, qw = q.shape
    kw, page = cache_kt.shape[2], cache_kt.shape[3]
    per_block = MOBA_BLOCK // page
    past = n_pages * page
    n_blocks = past // MOBA_BLOCK
    assert MOBA_BLOCK % page == 0 and pp % per_block == 0 and n_pages % pp == 0 and past % MOBA_BLOCK == 0
    assert n_blocks < LANES // 2 and t <= LANES and (t & (t - 1)) == 0 and t % SUBLANES == 0 and B_GROUP == 2
    assert page == LANES and kw == B_KV_HEADS * HEAD_DIM
    rows = B_HEADS * t
    per_seq = lambda w: pl.BlockSpec((None, t, w), lambda b, g, pt: (b, 0, 0))
    grid_spec = pltpu.PrefetchScalarGridSpec(
        num_scalar_prefetch=1,
        grid=(n, n_pages // pp),
        in_specs=[per_seq(qw), per_seq(kw), per_seq(kw)]
        + _paged_specs(layer, pp, n_pages, (kw, page)) + _paged_specs(layer, pp, n_pages, (kw, page)),
        out_specs=per_seq(qw),
        scratch_shapes=[pltpu.VMEM((rows, kw), F32),
                        pltpu.VMEM((rows, LANES), F32), pltpu.VMEM((rows, LANES), F32),
                        pltpu.VMEM((rows, LANES), F32), pltpu.VMEM((n_blocks, rows, kw), F32)],
    )
    return pl.pallas_call(
        functools.partial(_moba_decode_kernel, pp=pp, n_blocks=n_blocks),
        grid_spec=grid_spec,
        out_shape=jax.ShapeDtypeStruct((n, t, qw), F32),
        compiler_params=_cparams(("arbitrary", "arbitrary"), VMEM_LIMIT),
        name="moba_attn_decode",
    )(pt_flat, q, kn, vn, *([cache_kt] * pp), *([cache_vt] * pp))


G_SPLITS = (
    (0, A_HEADS * 2 * HEAD_DIM, True), (512, A_KV_HEADS * 2 * HEAD_DIM, True), (768, A_KV_HEADS * 2 * HEAD_DIM, False),
    (1024, B_HEADS * HEAD_DIM, True), (1536, B_KV_HEADS * HEAD_DIM, True), (1792, B_KV_HEADS * HEAD_DIM, False),
)
L_SPLITS = ((0, C_HEADS * HEAD_DIM, True), (1024, C_KV_HEADS * HEAD_DIM, True), (1152, C_KV_HEADS * HEAD_DIM, False))


def kernel(x_prompt, x_sample, cache_a_k, cache_a_v, cache_b_k, cache_b_v, state_c_k, state_c_v, state_ffn,
           page_table, norm_mix, w_in_g, w_out_g, diff_lambda, diff_subln, w_in_l, w_out_l, sinks, norm_ffn,
           w_up, conv_w, conv_b, w_down, norm_final):
    bsz, seq, d = x_prompt.shape
    nd, t_dec, _ = x_sample.shape
    depth = norm_mix.shape[0]
    ng, pool, page = cache_a_k.shape[:3]
    n_pages = page_table.shape[1]
    past = n_pages * page
    buf = state_c_k.shape[2]
    dff2 = w_up.shape[2]

    cast = lambda w: w.astype(MXU_DTYPE)
    w_in_g, w_out_g, w_in_l, w_out_l, w_up_c, w_down_c = map(cast, (w_in_g, w_out_g, w_in_l, w_out_l, w_up, w_down))

    kw_a = A_KV_HEADS * 2 * HEAD_DIM
    kw_b = B_KV_HEADS * HEAD_DIM
    cakt = jnp.transpose(cache_a_k, (0, 1, 3, 4, 5, 2)).reshape(ng, pool, kw_a, page)
    cav = cache_a_v.reshape(ng, pool, page * A_KV_HEADS, 2 * HEAD_DIM)
    cbkt = jnp.transpose(cache_b_k, (0, 1, 3, 4, 2)).reshape(ng, pool, kw_b, page)
    cbvt = jnp.transpose(cache_b_v, (0, 1, 3, 4, 2)).reshape(ng, pool, kw_b, page)
    pt_flat = page_table.reshape(-1)

    tab_p = _rope_tables(jnp.arange(seq, dtype=jnp.int32))
    pos_s = past + jnp.repeat(jnp.arange(t_dec, dtype=jnp.int32), nd)
    tab_s = _rope_tables(pos_s)

    n_s = nd * t_dec
    xp = x_prompt.reshape(bsz * seq, d)
    xs = jnp.swapaxes(x_sample, 0, 1).reshape(n_s, d)

    tm_p = _tile(seq, 512)
    tm_s = _tile(n_s, 512)
    tf = 256 if (dff2 // 2) % 256 == 0 else LANES
    halo_p = SUBLANES
    halo_s = (CONV_W - 1) * nd
    pp = 32 if n_pages % 32 == 0 else n_pages
    dec_blk = 8 if nd % 8 == 0 else nd
    win_blocks = 4 if (seq // WINDOW) % 4 == 0 else 1

    def to_seq_major(a):
        return jnp.swapaxes(a.reshape(t_dec, nd, -1), 0, 1)

    def to_pos_major(a):
        return jnp.swapaxes(a, 0, 1).reshape(n_s, -1)

    outs = {k: [] for k in ("akp", "avp", "bkp", "bvp", "ckp", "cvp", "fp", "aks", "avs", "bks", "bvs", "cks", "cvs", "fs")}
    op = os_ = None
    w_prev = None
    for l in range(depth):
        i = l // 2
        if l % 2 == 0:
            lam_init = 0.8 - 0.6 * math.exp(-0.3 * l)
            qa, ka, va, qb, kb, vb = _proj_call(xp, norm_mix[l], w_in_g[i], tab_p, G_SPLITS, tm_p)
            r3 = lambda a: a.reshape(bsz, seq, -1)
            feature_major = lambda a: jnp.swapaxes(r3(a), 1, 2)
            oa = _diff_prompt_call(r3(qa), feature_major(ka), r3(va), diff_lambda[i], diff_subln[i], lam_init,
                                   MOBA_BLOCK)
            ob = _moba_prompt_call(r3(qb), feature_major(kb), r3(vb))
            op = jnp.concatenate([oa, ob], axis=-1).reshape(bsz * seq, -1)
            outs["akp"].append(ka.reshape(bsz, seq, A_KV_HEADS, 2, HEAD_DIM))
            outs["avp"].append(va.reshape(bsz, seq, A_KV_HEADS, 2 * HEAD_DIM))
            outs["bkp"].append(kb.reshape(bsz, seq, B_KV_HEADS, HEAD_DIM))
            outs["bvp"].append(vb.reshape(bsz, seq, B_KV_HEADS, HEAD_DIM))

            qa, ka, va, qb, kb, vb = map(to_seq_major, _proj_call(xs, norm_mix[l], w_in_g[i], tab_s, G_SPLITS, tm_s))
            oa = _diff_decode_call(qa, ka, va, cakt, cav, i, pt_flat, n_pages, diff_lambda[i], diff_subln[i],
                                   lam_init, pp)
            ob = _moba_decode_call(qb, kb, vb, cbkt, cbvt, i, pt_flat, n_pages, pp)
            os_ = to_pos_major(jnp.concatenate([oa, ob], axis=-1))
            outs["aks"].append(ka.reshape(nd, t_dec, A_KV_HEADS, 2, HEAD_DIM))
            outs["avs"].append(va.reshape(nd, t_dec, A_KV_HEADS, 2 * HEAD_DIM))
            outs["bks"].append(kb.reshape(nd, t_dec, B_KV_HEADS, HEAD_DIM))
            outs["bvs"].append(vb.reshape(nd, t_dec, B_KV_HEADS, HEAD_DIM))
            w_o = w_out_g[i]
        else:
            q, k, v = _proj_call(xp, norm_mix[l], w_in_l[i], tab_p, L_SPLITS, tm_p)
            r3 = lambda a: a.reshape(bsz, seq, -1)
            op = _local_prompt_call(r3(q), r3(k), r3(v), sinks[i], win_blocks)
            op = op.reshape(bsz * seq, -1)
            outs["ckp"].append(r3(k)[:, seq - buf:].reshape(bsz, buf, C_KV_HEADS, HEAD_DIM))
            outs["cvp"].append(r3(v)[:, seq - buf:].reshape(bsz, buf, C_KV_HEADS, HEAD_DIM))

            q, k, v = map(to_seq_major, _proj_call(xs, norm_mix[l], w_in_l[i], tab_s, L_SPLITS, tm_s))
            o, ck, cv = _local_decode_call(q, state_c_k[i].reshape(nd, buf, LANES), state_c_v[i].reshape(nd, buf, LANES),
                                           k, v, sinks[i], dec_blk)
            os_ = to_pos_major(o)
            outs["cks"].append(ck.reshape(nd, buf, C_KV_HEADS, HEAD_DIM))
            outs["cvs"].append(cv.reshape(nd, buf, C_KV_HEADS, HEAD_DIM))
            w_o = w_out_l[i]

        final = l == depth - 1
        prev_p = jnp.zeros((bsz, halo_p, dff2), F32)
        xp, tail_p = _ffn_call(xp, op, w_o, norm_ffn[l], prev_p, w_up_c[l], conv_w[l], conv_b[l], w_down_c[l],
                               norm_final, seq_rows=seq, halo=halo_p, shift=1, final_norm=final, tm=tm_p, tf=tf)
        outs["fp"].append(tail_p[:, halo_p - (CONV_W - 1):])
        prev_s = jnp.swapaxes(state_ffn[l], 0, 1).reshape(1, halo_s, dff2)
        xs, tail_s = _ffn_call(xs, os_, w_o, norm_ffn[l], prev_s, w_up_c[l], conv_w[l], conv_b[l], w_down_c[l],
                               norm_final, seq_rows=n_s, halo=halo_s, shift=nd, final_norm=final, tm=tm_s, tf=tf)
        outs["fs"].append(jnp.swapaxes(tail_s.reshape(CONV_W - 1, nd, dff2), 0, 1))

    st = lambda key: jnp.stack(outs[key])
    y_p = xp.reshape(bsz, seq, d)
    y_s = jnp.swapaxes(xs.reshape(t_dec, nd, d), 0, 1)
    return (y_p, y_s, st("akp"), st("avp"), st("bkp"), st("bvp"), st("ckp"), st("cvp"), st("fp"),
            st("aks"), st("avs"), st("bks"), st("bvs"), st("cks"), st("cvs"), st("fs"))
```

```python
import functools
import math

import jax
import jax.numpy as jnp
from jax import lax
from jax.experimental import pallas as pl
from jax.experimental.pallas import tpu as pltpu

HEAD_DIM = 64
SCALE = HEAD_DIM ** -0.5
ROT_DIM = HEAD_DIM // 4
ROPE_THETA = 500000.0
NORM_EPS = 1e-6
SUBLN_EPS = 1e-5
A_HEADS, A_KV_HEADS = 4, 2
A_GROUP = A_HEADS // A_KV_HEADS
B_HEADS, B_KV_HEADS = 8, 4
B_GROUP = B_HEADS // B_KV_HEADS
MOBA_BLOCK = 256
MOBA_TOPK = 3
C_HEADS, C_KV_HEADS = 16, 2
C_GROUP = C_HEADS // C_KV_HEADS
WINDOW = 128
CONV_W = 3

LANES = 128
SUBLANES = 8
HALF = LANES // 2
NEG = -1e30
ROW_CHUNK = 64
MXU_DTYPE = jnp.bfloat16
VMEM_LIMIT = 52 * 1024 * 1024

F32 = jnp.float32


def _cparams(sem, vmem=None):
    return pltpu.CompilerParams(dimension_semantics=sem, vmem_limit_bytes=vmem)


def _tile(n, pref):
    t = min(n, pref)
    while n % t or t % SUBLANES:
        t -= 1
    return t


def _lane_iota(shape):
    return lax.broadcasted_iota(jnp.int32, shape, len(shape) - 1)


def _row_iota(shape):
    return lax.broadcasted_iota(jnp.int32, shape, len(shape) - 2)


def _half_place(chunk, src_half, dst_half):
    x = chunk if src_half == dst_half else pltpu.roll(chunk, HALF, axis=1)
    lane = _lane_iota(x.shape)
    keep = (lane >= HALF) if dst_half else (lane < HALF)
    return jnp.where(keep, x, 0.0)


def _merge_halves(lo, lo_src_half, hi, hi_src_half):
    a = lo if lo_src_half == 0 else pltpu.roll(lo, HALF, axis=1)
    b = hi if hi_src_half == 1 else pltpu.roll(hi, HALF, axis=1)
    return jnp.where(_lane_iota(a.shape) < HALF, a, b)


def _dot_nt(a, b):
    return lax.dot_general(a, b, (((1,), (1,)), ((), ())), preferred_element_type=F32)


def _dot(a, b):
    return jnp.dot(a, b, preferred_element_type=F32)


def _rms(x, g, eps):
    ms = jnp.mean(x * x, axis=-1, keepdims=True)
    return x * lax.rsqrt(ms + eps) * g


def _gelu(x):
    return 0.5 * x * (1.0 + lax.erf(x * math.sqrt(0.5)))


def _diff_lambda(lp, lam_init):
    a = jnp.sum(lp[0:1] * lp[1:2], axis=-1, keepdims=True)
    b = jnp.sum(lp[2:3] * lp[3:4], axis=-1, keepdims=True)
    return jnp.exp(a) - jnp.exp(b) + lam_init


def _rope_tables(pos):
    half = ROT_DIM // 2
    inv = ROPE_THETA ** (-jnp.arange(half, dtype=F32) * 2.0 / ROT_DIM)
    ang = pos.astype(F32)[:, None] * inv[None, :]
    cos, sin = jnp.cos(ang), jnp.sin(ang)
    t = pos.shape[0]
    rest = HEAD_DIM - ROT_DIM
    zh = jnp.zeros((t, half), F32)
    c = jnp.concatenate([cos, cos, jnp.ones((t, rest), F32)], axis=-1)
    sn = jnp.concatenate([-sin, zh, jnp.zeros((t, rest), F32)], axis=-1)
    sp = jnp.concatenate([zh, sin, jnp.zeros((t, rest), F32)], axis=-1)
    rep = LANES // HEAD_DIM
    return tuple(jnp.tile(a, (1, rep)) for a in (c, sn, sp))


def _proj_kernel(x_ref, g_ref, w_ref, cos_ref, sn_ref, sp_ref, *out_refs, splits):
    xn = _rms(x_ref[...], g_ref[...], NORM_EPS).astype(MXU_DTYPE)
    cos, sn, sp = cos_ref[...], sn_ref[...], sp_ref[...]
    half = ROT_DIM // 2
    for o_ref, (start, width, rotary) in zip(out_refs, splits):
        y = _dot(xn, w_ref[:, start:start + width])
        for c in range(width // LANES):
            yc = y[:, c * LANES:(c + 1) * LANES]
            if rotary:
                yc = (yc * cos + pltpu.roll(yc, LANES - half, axis=1) * sn
                      + pltpu.roll(yc, half, axis=1) * sp)
            o_ref[:, c * LANES:(c + 1) * LANES] = yc


def _proj_call(x, g, w, tables, splits, tm):
    n, d = x.shape
    nt = tables[0].shape[0]
    assert n % tm == 0 and nt % tm == 0
    ntb = nt // tm
    row = lambda i: (i, 0)
    fixed = lambda i: (0, 0)
    tab = lambda i: (i % ntb, 0)
    in_specs = [pl.BlockSpec((tm, d), row), pl.BlockSpec((1, d), fixed), pl.BlockSpec(w.shape, fixed)]
    in_specs += [pl.BlockSpec((tm, LANES), tab)] * 3
    return pl.pallas_call(
        functools.partial(_proj_kernel, splits=splits),
        grid=(n // tm,),
        in_specs=in_specs,
        out_specs=[pl.BlockSpec((tm, wd), row) for _, wd, _ in splits],
        out_shape=[jax.ShapeDtypeStruct((n, wd), F32) for _, wd, _ in splits],
        compiler_params=_cparams(("arbitrary",), VMEM_LIMIT),
        name="norm_proj_rope",
    )(x, g.reshape(1, d), w, *tables)


def _ffn_kernel(x_ref, o_ref, wout_ref, g_ref, pa_ref, pb_ref, wa_ref, wb_ref, cwa_ref, cwb_ref,
                cba_ref, cbb_ref, wd_ref, gf_ref, xo_ref, sa_ref, sb_ref,
                x1_s, hn_s, acc_s, ua_s, ub_s, ha_s, hb_s, *, tiles_per_seq, halo, shift, final_norm):
    i, j = pl.program_id(0), pl.program_id(1)
    nj = pl.num_programs(1)
    tm = x_ref.shape[0]

    @pl.when(j == 0)
    def _():
        x1 = x_ref[...] + _dot(o_ref[...].astype(MXU_DTYPE), wout_ref[...])
        x1_s[...] = x1
        hn_s[...] = _rms(x1, g_ref[...], NORM_EPS).astype(MXU_DTYPE)
        acc_s[...] = jnp.zeros_like(acc_s)

    first = (i % tiles_per_seq) == 0

    @pl.when(first)
    def _():
        ua_s[0:halo] = pa_ref[...]
        ub_s[0:halo] = pb_ref[...]

    @pl.when(jnp.logical_not(first))
    def _():
        ua_s[0:halo] = ha_s[j]
        ub_s[0:halo] = hb_s[j]

    hn = hn_s[...]
    ua_s[halo:halo + tm] = _dot(hn, wa_ref[...])
    ub_s[halo:halo + tm] = _dot(hn, wb_ref[...])

    def conv(u_s, cw_ref, cb_ref):
        c = cb_ref[...]
        for tap in range(CONV_W):
            off = halo - (CONV_W - 1 - tap) * shift
            c = c + u_s[off:off + tm] * cw_ref[tap:tap + 1]
        return c

    a = conv(ua_s, cwa_ref, cba_ref)
    b = conv(ub_s, cwb_ref, cbb_ref)
    act = (_gelu(a) * b).astype(MXU_DTYPE)
    acc_s[...] += _dot(act, wd_ref[...])

    tail_a = ua_s[tm:tm + halo]
    tail_b = ub_s[tm:tm + halo]
    ha_s[j] = tail_a
    hb_s[j] = tail_b
    sa_ref[...] = tail_a
    sb_ref[...] = tail_b

    @pl.when(j == nj - 1)
    def _():
        xo = x1_s[...] + acc_s[...]
        if final_norm:
            xo = _rms(xo, gf_ref[...], NORM_EPS)
        xo_ref[...] = xo


def _ffn_call(x, o, w_out, g, prev, w_up, conv_w, conv_b, w_down, g_final, *, seq_rows, halo, shift,
              final_norm, tm, tf):
    n, d = x.shape
    dff = w_down.shape[0]
    assert seq_rows % tm == 0 and dff % tf == 0 and tm >= halo
    tiles_per_seq = seq_rows // tm
    nseq = n // seq_rows
    nj = dff // tf
    row = lambda i, j: (i, 0)
    fixed = lambda i, j: (0, 0)
    in_specs = [
        pl.BlockSpec((tm, d), row), pl.BlockSpec((tm, d), row),
        pl.BlockSpec(w_out.shape, fixed), pl.BlockSpec((1, d), fixed),
        pl.BlockSpec((None, halo, tf), lambda i, j: (i // tiles_per_seq, 0, j)),
        pl.BlockSpec((None, halo, tf), lambda i, j: (i // tiles_per_seq, 0, nj + j)),
        pl.BlockSpec((None, d, tf), lambda i, j: (j, 0, 0)), pl.BlockSpec((None, d, tf), lambda i, j: (nj + j, 0, 0)),
        pl.BlockSpec((CONV_W, tf), lambda i, j: (0, j)), pl.BlockSpec((CONV_W, tf), lambda i, j: (0, nj + j)),
        pl.BlockSpec((1, tf), lambda i, j: (0, j)), pl.BlockSpec((1, tf), lambda i, j: (0, nj + j)),
        pl.BlockSpec((tf, d), lambda i, j: (j, 0)), pl.BlockSpec((1, d), fixed),
    ]
    tail_spec = pl.BlockSpec((None, halo, tf), lambda i, j: (i, 0, j))
    xo, ta, tb = pl.pallas_call(
        functools.partial(_ffn_kernel, tiles_per_seq=tiles_per_seq, halo=halo, shift=shift,
                          final_norm=final_norm),
        grid=(n // tm, nj),
        in_specs=in_specs,
        out_specs=[pl.BlockSpec((tm, d), row), tail_spec, tail_spec],
        out_shape=[jax.ShapeDtypeStruct((n, d), F32), jax.ShapeDtypeStruct((n // tm, halo, dff), F32),
                   jax.ShapeDtypeStruct((n // tm, halo, dff), F32)],
        scratch_shapes=[
            pltpu.VMEM((tm, d), F32), pltpu.VMEM((tm, d), MXU_DTYPE), pltpu.VMEM((tm, d), F32),
            pltpu.VMEM((halo + tm, tf), F32), pltpu.VMEM((halo + tm, tf), F32),
            pltpu.VMEM((nj, halo, tf), F32), pltpu.VMEM((nj, halo, tf), F32),
        ],
        compiler_params=_cparams(("arbitrary", "arbitrary"), VMEM_LIMIT),
        name="outproj_convffn",
    )(x, o, w_out, g.reshape(1, d), prev, prev, w_up, w_up, conv_w, conv_w,
      conv_b.reshape(1, -1), conv_b.reshape(1, -1), w_down, g_final.reshape(1, d))
    last = slice(tiles_per_seq - 1, None, tiles_per_seq)
    return xo, jnp.concatenate([ta[last], tb[last]], axis=-1)


def _flash_update(qs_s, kt, v, m_s, l_s, acc_s, mask_fn):
    tk = kt.shape[1]
    for r0 in range(0, qs_s.shape[0], ROW_CHUNK):
        rs = slice(r0, r0 + ROW_CHUNK)
        s = _dot(qs_s[rs], kt)
        if mask_fn is not None:
            s = mask_fn(r0, s)
        parts = [s[:, i * LANES:(i + 1) * LANES] for i in range(tk // LANES)]
        mx = jnp.max(functools.reduce(jnp.maximum, parts), axis=-1, keepdims=True)
        m_prev = m_s[rs]
        m_new = jnp.maximum(m_prev, mx)
        alpha = jnp.exp(m_prev - m_new)
        ps = [jnp.exp(p - m_new) for p in parts]
        l_s[rs] = alpha * l_s[rs] + jnp.sum(functools.reduce(jnp.add, ps), axis=-1, keepdims=True)
        acc_s[rs] = alpha * acc_s[rs] + _dot(jnp.concatenate(ps, axis=1).astype(MXU_DTYPE), v)
        m_s[rs] = m_new


def _causal_mask(tq):
    def mask(r0, s):
        r = (_row_iota(s.shape) + r0) & (tq - 1)
        return jnp.where(_lane_iota(s.shape) <= r, s, NEG)
    return mask


def _diff_finish(acc, l, lam, g, lam_init, rows):
    o = acc[0:rows] / l[0:rows] - lam * (acc[rows:2 * rows] / l[rows:2 * rows])
    return _rms(o, g, SUBLN_EPS) * (1.0 - lam_init)


def _diff_prompt_kernel(q_ref, kt_ref, v_ref, lam_ref, g_ref, o_ref, qs_s, m_s, l_s, acc_s, *, lam_init):
    qi = pl.program_id(2)
    tq = q_ref.shape[0]
    rows = A_GROUP * tq
    q = q_ref[...]
    lane = _lane_iota((tq, LANES))
    for m in range(2):
        for g in range(A_GROUP):
            chunk = q[:, g * LANES:(g + 1) * LANES]
            keep = (lane >= HALF) if m else (lane < HALF)
            r0 = (m * A_GROUP + g) * tq
            qs_s[r0:r0 + tq] = (jnp.where(keep, chunk, 0.0) * SCALE).astype(MXU_DTYPE)
    m_s[...] = jnp.full_like(m_s, NEG)
    l_s[...] = jnp.zeros_like(l_s)
    acc_s[...] = jnp.zeros_like(acc_s)

    def step(j, mask_fn):
        start = pl.multiple_of(j * tq, tq)
        kt = kt_ref[:, pl.ds(start, tq)].astype(MXU_DTYPE)
        v = v_ref[pl.ds(start, tq), :].astype(MXU_DTYPE)
        _flash_update(qs_s, kt, v, m_s, l_s, acc_s, mask_fn)

    step(qi, _causal_mask(tq))
    lax.fori_loop(0, qi, lambda j, c: (step(j, None), c)[1], 0)

    lam = _diff_lambda(lam_ref[...], lam_init)
    o = _diff_finish(acc_s[...], l_s[...], lam, g_ref[...], lam_init, rows)
    for g in range(A_GROUP):
        o_ref[:, g * LANES:(g + 1) * LANES] = o[g * tq:(g + 1) * tq]


def _diff_prompt_call(qa, kat, va, lam_p, subln_g, lam_init, tq):
    b, s, _ = qa.shape
    assert s % tq == 0 and (tq & (tq - 1)) == 0 and (2 * A_GROUP * tq) % ROW_CHUNK == 0
    kvw = 2 * HEAD_DIM
    qw = A_GROUP * kvw
    rows = 2 * A_GROUP * tq
    return pl.pallas_call(
        functools.partial(_diff_prompt_kernel, lam_init=lam_init),
        grid=(b, A_KV_HEADS, s // tq),
        in_specs=[
            pl.BlockSpec((None, tq, qw), lambda bi, h, i: (bi, i, h)),
            pl.BlockSpec((None, kvw, s), lambda bi, h, i: (bi, h, 0)),
            pl.BlockSpec((None, s, kvw), lambda bi, h, i: (bi, 0, h)),
            pl.BlockSpec(lam_p.shape, lambda bi, h, i: (0, 0)),
            pl.BlockSpec((1, kvw), lambda bi, h, i: (0, 0)),
        ],
        out_specs=pl.BlockSpec((None, tq, qw), lambda bi, h, i: (bi, i, h)),
        out_shape=jax.ShapeDtypeStruct((b, s, A_KV_HEADS * qw), F32),
        scratch_shapes=[pltpu.VMEM((rows, LANES), MXU_DTYPE), pltpu.VMEM((rows, LANES), F32),
                        pltpu.VMEM((rows, LANES), F32), pltpu.VMEM((rows, LANES), F32)],
        compiler_params=_cparams(("arbitrary", "arbitrary", "arbitrary"), VMEM_LIMIT),
        name="diff_attn_prompt",
    )(qa, kat, va, lam_p, subln_g.reshape(1, kvw))


def _topk_select_rows(gate_t, valid, n):
    gm = jnp.where(valid, gate_t, -jnp.inf)
    idx = _row_iota(gm.shape)
    rank = jnp.zeros(gm.shape, F32)
    for i in range(n):
        gi = gm[i:i + 1, :]
        beats = (gi > gm) | ((gi == gm) & (idx > i))
        rank = rank + beats.astype(F32)
    return valid & (rank < MOBA_TOPK)


def _moba_prompt_kernel(q_ref, kt_ref, v_ref, o_ref, qs_s, qf_s, km_s, sel_s, m_s, l_s, acc_s, *, nb):
    qi = pl.program_id(2)
    tq = q_ref.shape[0]
    nbp = -(-nb // SUBLANES) * SUBLANES
    q = q_ref[...]
    for kvp in range(2):
        for g in range(B_GROUP):
            piece = _half_place(q[:, kvp * LANES:(kvp + 1) * LANES], g, kvp)
            r0 = (kvp * B_GROUP + g) * tq
            qf_s[r0:r0 + tq] = piece
            qs_s[r0:r0 + tq] = (piece * SCALE).astype(MXU_DTYPE)

    @pl.when(qi == 0)
    def _():
        col_block = _lane_iota((LANES, LANES)) & (nb - 1)
        km = jnp.zeros((LANES, LANES), F32)
        for n in range(nb):
            mean_n = jnp.sum(kt_ref[:, n * MOBA_BLOCK:(n + 1) * MOBA_BLOCK], axis=-1, keepdims=True) * (1.0 / MOBA_BLOCK)
            km = jnp.where(col_block == n, mean_n, km)
        km_s[...] = km

    m_s[...] = jnp.full_like(m_s, NEG)
    l_s[...] = jnp.zeros_like(l_s)
    acc_s[...] = jnp.zeros_like(acc_s)

    gate = jnp.dot(qf_s[...], km_s[...], precision=lax.Precision.HIGHEST, preferred_element_type=F32)
    gate_t = jnp.transpose(gate)[0:nbp]
    sel_t = _topk_select_rows(gate_t, _row_iota(gate_t.shape) < jnp.minimum(qi, nb), nb).astype(F32)
    sel_t = jnp.concatenate([sel_t, jnp.zeros((LANES - nbp, gate.shape[0]), F32)], axis=0)
    sel_s[...] = jnp.transpose(sel_t)

    def block_step(back, carry):
        j = qi - back
        start = pl.multiple_of(j * tq, tq)
        kt = kt_ref[:, pl.ds(start, tq)].astype(MXU_DTYPE)
        v = v_ref[pl.ds(start, tq), :].astype(MXU_DTYPE)

        def mask(r0, s):
            sel = sel_s[r0:r0 + ROW_CHUNK]
            picked = jnp.sum(jnp.where(_lane_iota(sel.shape) == j, sel, 0.0), axis=-1, keepdims=True)
            own_limit = (_row_iota((ROW_CHUNK, 1)) + r0) & (tq - 1)
            limit = jnp.where(back == 0, own_limit, jnp.where(picked > 0.0, tq, -1))
            return jnp.where(_lane_iota(s.shape) <= limit, s, NEG)

        _flash_update(qs_s, kt, v, m_s, l_s, acc_s, mask)
        return carry

    lax.fori_loop(0, qi + 1, block_step, 0)

    o = acc_s[...] / l_s[...]
    for kvp in range(2):
        r0 = kvp * B_GROUP * tq
        o_ref[:, kvp * LANES:(kvp + 1) * LANES] = _merge_halves(o[r0:r0 + tq], kvp, o[r0 + tq:r0 + 2 * tq], kvp)


def _moba_prompt_call(qb, kbt, vb):
    b, s, _ = qb.shape
    tq = MOBA_BLOCK
    nb = s // tq
    assert s % tq == 0 and LANES % nb == 0 and B_GROUP == 2
    pairs = B_KV_HEADS // 2
    qw = 2 * B_GROUP * HEAD_DIM
    rows = 2 * B_GROUP * tq
    assert rows % ROW_CHUNK == 0
    return pl.pallas_call(
        functools.partial(_moba_prompt_kernel, nb=nb),
        grid=(b, pairs, nb),
        in_specs=[
            pl.BlockSpec((None, tq, qw), lambda bi, h, i: (bi, i, h)),
            pl.BlockSpec((None, LANES, s), lambda bi, h, i: (bi, h, 0)),
            pl.BlockSpec((None, s, LANES), lambda bi, h, i: (bi, 0, h)),
        ],
        out_specs=pl.BlockSpec((None, tq, qw), lambda bi, h, i: (bi, i, h)),
        out_shape=jax.ShapeDtypeStruct((b, s, pairs * qw), F32),
        scratch_shapes=[pltpu.VMEM((rows, LANES), MXU_DTYPE), pltpu.VMEM((rows, LANES), F32),
                        pltpu.VMEM((LANES, LANES), F32), pltpu.VMEM((rows, LANES), F32),
                        pltpu.VMEM((rows, LANES), F32), pltpu.VMEM((rows, LANES), F32),
                        pltpu.VMEM((rows, LANES), F32)],
        compiler_params=_cparams(("arbitrary", "arbitrary", "arbitrary"), VMEM_LIMIT),
        name="moba_attn_prompt",
    )(qb, kbt, vb)


def _window_q(q, rows):
    pieces = []
    for h in range(C_HEADS):
        chunk = q[:, (h // 2) * LANES:(h // 2 + 1) * LANES]
        pieces.append(_half_place(chunk, h % 2, h // C_GROUP))
    return (jnp.concatenate(pieces, axis=0) * SCALE).astype(MXU_DTYPE)


def _sink_column(sink_ref, rows):
    return jnp.concatenate([jnp.full((rows, 1), sink_ref[h], F32) for h in range(C_HEADS)], axis=0)


def _window_out(o, rows):
    chunks = []
    for c in range(C_HEADS // 2):
        kv = (2 * c) // C_GROUP
        lo = o[(2 * c) * rows:(2 * c + 1) * rows]
        hi = o[(2 * c + 1) * rows:(2 * c + 2) * rows]
        chunks.append(_merge_halves(lo, kv, hi, kv))
    return chunks


def _sink_softmax_pv(s, sink, v):
    m = jnp.maximum(jnp.max(s, axis=-1, keepdims=True), sink)
    p = jnp.exp(s - m)
    den = jnp.sum(p, axis=-1, keepdims=True) + jnp.exp(sink - m)
    return _dot(p.astype(MXU_DTYPE), v) / den


def _local_prompt_kernel(sink_ref, q_ref, ktp_ref, ktc_ref, vp_ref, vc_ref, o_ref):
    has_prev = pl.program_id(1) > 0
    kt = jnp.concatenate([ktp_ref[...], ktc_ref[...]], axis=1).astype(MXU_DTYPE)
    v = jnp.concatenate([vp_ref[...], vc_ref[...]], axis=0).astype(MXU_DTYPE)
    col = _lane_iota((WINDOW, 2 * WINDOW))
    rel = WINDOW + _row_iota(col.shape) - col
    visible = (rel >= 0) & (rel < WINDOW) & ((col >= WINDOW) | has_prev)
    for c in range(C_HEADS // 2):
        kv = (2 * c) // C_GROUP
        chunk = q_ref[:, c * LANES:(c + 1) * LANES]
        outs = []
        for hh in range(2):
            sink = sink_ref[2 * c + hh]
            qh = (_half_place(chunk, hh, kv) * SCALE).astype(MXU_DTYPE)
            s = jnp.where(visible, _dot(qh, kt), NEG)
            parts = [s[:, i * LANES:(i + 1) * LANES] for i in range(2 * WINDOW // LANES)]
            m = jnp.maximum(jnp.max(functools.reduce(jnp.maximum, parts), axis=-1, keepdims=True), sink)
            ps = [jnp.exp(p - m) for p in parts]
            den = jnp.sum(functools.reduce(jnp.add, ps), axis=-1, keepdims=True) + jnp.exp(sink - m)
            outs.append(_dot(jnp.concatenate(ps, axis=1).astype(MXU_DTYPE), v) / den)
        o_ref[:, c * LANES:(c + 1) * LANES] = _merge_halves(outs[0], kv, outs[1], kv)


def _local_prompt_call(q, kt, v, sinks):
    b, s, qw = q.shape
    assert s % WINDOW == 0 and C_KV_HEADS * HEAD_DIM == LANES and WINDOW == LANES
    prev = lambda i: jnp.maximum(i - 1, 0)
    return pl.pallas_call(
        _local_prompt_kernel,
        grid=(b, s // WINDOW),
        in_specs=[
            pl.BlockSpec(memory_space=pltpu.SMEM),
            pl.BlockSpec((None, WINDOW, qw), lambda bi, i: (bi, i, 0)),
            pl.BlockSpec((None, LANES, WINDOW), lambda bi, i: (bi, 0, prev(i))),
            pl.BlockSpec((None, LANES, WINDOW), lambda bi, i: (bi, 0, i)),
            pl.BlockSpec((None, WINDOW, LANES), lambda bi, i: (bi, prev(i), 0)),
            pl.BlockSpec((None, WINDOW, LANES), lambda bi, i: (bi, i, 0)),
        ],
        out_specs=pl.BlockSpec((None, WINDOW, qw), lambda bi, i: (bi, i, 0)),
        out_shape=jax.ShapeDtypeStruct((b, s, qw), F32),
        compiler_params=_cparams(("arbitrary", "arbitrary"), VMEM_LIMIT),
        name="window_attn_prompt",
    )(sinks, q, kt, kt, v, v)


def _local_decode_kernel(sink_ref, q_ref, ck_ref, cv_ref, kn_ref, vn_ref, o_ref, ko_ref, vo_ref):
    nseq, t, _ = q_ref.shape
    buf = ck_ref.shape[1]
    sink = _sink_column(sink_ref, t)
    pad = jnp.zeros((2 * WINDOW - buf - t, LANES), F32)
    for sq in range(nseq):
        kn, vn = kn_ref[sq], vn_ref[sq]
        k = jnp.concatenate([ck_ref[sq], kn, pad], axis=0).astype(MXU_DTYPE)
        v = jnp.concatenate([cv_ref[sq], vn, pad], axis=0).astype(MXU_DTYPE)
        s = _dot_nt(_window_q(q_ref[sq], t), k)
        col = _lane_iota(s.shape)
        rel = buf + (_row_iota(s.shape) & (t - 1)) - col
        s = jnp.where((rel >= 0) & (rel < WINDOW) & (col < buf + t), s, NEG)
        o = _sink_softmax_pv(s, sink, v)
        for c, chunk in enumerate(_window_out(o, t)):
            o_ref[sq, :, c * LANES:(c + 1) * LANES] = chunk
        ko_ref[sq, 0:buf - t] = ck_ref[sq, t:buf]
        ko_ref[sq, buf - t:buf] = kn
        vo_ref[sq, 0:buf - t] = cv_ref[sq, t:buf]
        vo_ref[sq, buf - t:buf] = vn


def _local_decode_call(q, ck, cv, kn, vn, sinks, nseq):
    n, t, qw = q.shape
    buf = ck.shape[1]
    assert n % nseq == 0 and buf + t <= 2 * WINDOW and (t & (t - 1)) == 0 and t % SUBLANES == 0
    blk = lambda shape: pl.BlockSpec((nseq,) + shape, lambda i: (i, 0, 0))
    return pl.pallas_call(
        _local_decode_kernel,
        grid=(n // nseq,),
        in_specs=[pl.BlockSpec(memory_space=pltpu.SMEM), blk((t, qw)), blk((buf, LANES)), blk((buf, LANES)),
                  blk((t, LANES)), blk((t, LANES))],
        out_specs=[blk((t, qw)), blk((buf, LANES)), blk((buf, LANES))],
        out_shape=[jax.ShapeDtypeStruct((n, t, qw), F32), jax.ShapeDtypeStruct((n, buf, LANES), F32),
                   jax.ShapeDtypeStruct((n, buf, LANES), F32)],
        compiler_params=_cparams(("arbitrary",), VMEM_LIMIT),
        name="window_attn_decode",
    )(sinks, q, ck, cv, kn, vn)


def _diff_decode_kernel(pt_ref, q_ref, kn_ref, vn_ref, lam_ref, g_ref, *refs, pp, lam_init):
    kt_refs, v_refs = refs[:pp], refs[pp:2 * pp]
    o_ref = refs[2 * pp]
    qs_s, m_s, l_s, acc_s = refs[2 * pp + 1:]
    gi = pl.program_id(1)
    t = q_ref.shape[0]
    kw, page = kt_refs[0].shape
    gr = 2 * A_GROUP * t

    @pl.when(gi == 0)
    def _():
        q = q_ref[...]
        lane = _lane_iota((t, LANES))
        zero = jnp.zeros((t, LANES), F32)
        for kv in range(A_KV_HEADS):
            for m in range(2):
                for g in range(A_GROUP):
                    c = kv * A_GROUP + g
                    keep = (lane >= HALF) if m else (lane < HALF)
                    piece = jnp.where(keep, q[:, c * LANES:(c + 1) * LANES], 0.0) * SCALE
                    full = jnp.concatenate([piece if kk == kv else zero for kk in range(A_KV_HEADS)], axis=1)
                    r0 = ((kv * 2 + m) * A_GROUP + g) * t
                    qs_s[r0:r0 + t] = full
        m_s[...] = jnp.full_like(m_s, NEG)
        l_s[...] = jnp.zeros_like(l_s)
        acc_s[...] = jnp.zeros_like(acc_s)

    def update(s, values_of_head):
        parts = [s[:, i * LANES:(i + 1) * LANES] for i in range(s.shape[1] // LANES)]
        mx = jnp.max(functools.reduce(jnp.maximum, parts), axis=-1, keepdims=True)
        m_prev = m_s[...]
        m_new = jnp.maximum(m_prev, mx)
        alpha = jnp.exp(m_prev - m_new)
        ps = [jnp.exp(p - m_new) for p in parts]
        l_s[...] = alpha * l_s[...] + jnp.sum(functools.reduce(jnp.add, ps), axis=-1, keepdims=True)
        p = jnp.concatenate(ps, axis=1).astype(MXU_DTYPE)
        pv = jnp.concatenate([_dot(p[kv * gr:(kv + 1) * gr], values_of_head(kv)) for kv in range(A_KV_HEADS)], axis=0)
        acc_s[...] = alpha * acc_s[...] + pv
        m_s[...] = m_new

    def paged_values(kv):
        return jnp.concatenate([vr[pl.ds(kv, page, stride=A_KV_HEADS), :].astype(MXU_DTYPE) for vr in v_refs], axis=0)

    qs = qs_s[...].astype(MXU_DTYPE)
    kt = jnp.concatenate([kr[...].astype(MXU_DTYPE) for kr in kt_refs], axis=1)
    update(_dot(qs, kt), paged_values)

    @pl.when(gi == pl.num_programs(1) - 1)
    def _():
        pad = jnp.zeros((LANES - t, kw), F32)
        kn = jnp.concatenate([kn_ref[...], pad], axis=0).astype(MXU_DTYPE)
        vn = jnp.concatenate([vn_ref[...], pad], axis=0).astype(MXU_DTYPE)
        sn = _dot_nt(qs, kn)
        sn = jnp.where(_lane_iota(sn.shape) <= (_row_iota(sn.shape) & (t - 1)), sn, NEG)
        update(sn, lambda kv: vn[:, kv * LANES:(kv + 1) * LANES])
        lam = _diff_lambda(lam_ref[...], lam_init)
        acc, l = acc_s[...], l_s[...]
        for kv in range(A_KV_HEADS):
            o = _diff_finish(acc[kv * gr:(kv + 1) * gr], l[kv * gr:(kv + 1) * gr], lam, g_ref[...], lam_init,
                             A_GROUP * t)
            for g in range(A_GROUP):
                c = kv * A_GROUP + g
                o_ref[:, c * LANES:(c + 1) * LANES] = o[g * t:(g + 1) * t]


def _paged_specs(layer, pp, n_pages, shape):
    def make(i):
        return pl.BlockSpec((None, None) + shape, lambda b, g, pt: (layer, pt[b * n_pages + g * pp + i], 0, 0))
    return [make(i) for i in range(pp)]


def _diff_decode_call(q, kn, vn, cache_kt, cache_v, layer, pt_flat, n_pages, lam_p, subln_g, lam_init, pp):
    n, t, qw = q.shape
    kw, page = cache_kt.shape[2], cache_kt.shape[3]
    assert n_pages % pp == 0 and (t & (t - 1)) == 0 and t % SUBLANES == 0 and t <= LANES and page == LANES
    assert cache_v.shape[2:] == (page * A_KV_HEADS, 2 * HEAD_DIM)
    rows = A_KV_HEADS * 2 * A_GROUP * t
    per_seq = lambda w: pl.BlockSpec((None, t, w), lambda b, g, pt: (b, 0, 0))
    fixed = lambda shape: pl.BlockSpec(shape, lambda b, g, pt: (0, 0))
    grid_spec = pltpu.PrefetchScalarGridSpec(
        num_scalar_prefetch=1,
        grid=(n, n_pages // pp),
        in_specs=[per_seq(qw), per_seq(kw), per_seq(kw), fixed(lam_p.shape), fixed((1, LANES))]
        + _paged_specs(layer, pp, n_pages, (kw, page))
        + _paged_specs(layer, pp, n_pages, (page * A_KV_HEADS, 2 * HEAD_DIM)),
        out_specs=per_seq(qw),
        scratch_shapes=[pltpu.VMEM((rows, kw), F32), pltpu.VMEM((rows, LANES), F32),
                        pltpu.VMEM((rows, LANES), F32), pltpu.VMEM((rows, LANES), F32)],
    )
    return pl.pallas_call(
        functools.partial(_diff_decode_kernel, pp=pp, lam_init=lam_init),
        grid_spec=grid_spec,
        out_shape=jax.ShapeDtypeStruct((n, t, qw), F32),
        compiler_params=_cparams(("arbitrary", "arbitrary"), VMEM_LIMIT),
        name="diff_attn_decode",
    )(pt_flat, q, kn, vn, lam_p, subln_g.reshape(1, LANES), *([cache_kt] * pp), *([cache_v] * pp))


def _moba_decode_kernel(pt_ref, q_ref, kn_ref, vn_ref, *refs, pp, n_blocks):
    kt_refs, vt_refs = refs[:pp], refs[pp:2 * pp]
    o_ref = refs[2 * pp]
    qf_s, m_s, l_s, g_s, o_s = refs[2 * pp + 1:]
    gi = pl.program_id(1)
    t = q_ref.shape[0]
    kw, page = kt_refs[0].shape
    per_block = MOBA_BLOCK // page
    rows = B_HEADS * t

    @pl.when(gi == 0)
    def _():
        q = q_ref[...]
        zero = jnp.zeros((t, LANES), F32)
        for kv in range(B_KV_HEADS):
            for g in range(B_GROUP):
                piece = _half_place(q[:, kv * LANES:(kv + 1) * LANES], g, kv % 2)
                full = jnp.concatenate([piece if c == kv // 2 else zero for c in range(kw // LANES)], axis=1)
                r0 = (kv * B_GROUP + g) * t
                qf_s[r0:r0 + t] = full
        m_s[...] = jnp.full_like(m_s, NEG)
        l_s[...] = jnp.zeros_like(l_s)
        g_s[...] = jnp.zeros_like(g_s)

    qs = (qf_s[...] * SCALE).astype(MXU_DTYPE)
    lane = _lane_iota((rows, LANES))
    n_here = pp // per_block
    block_pages = lambda refs, bi: jnp.concatenate(
        [refs[i][...].astype(MXU_DTYPE) for i in range(bi * per_block, (bi + 1) * per_block)], axis=1)

    scores = [_dot(qs, block_pages(kt_refs, bi)) for bi in range(n_here)]
    parts = [[s[:, i * LANES:(i + 1) * LANES] for i in range(MOBA_BLOCK // LANES)] for s in scores]
    maxes = [jnp.max(functools.reduce(jnp.maximum, pt), axis=-1, keepdims=True) for pt in parts]
    probs = [[jnp.exp(p - mb) for p in pt] for pt, mb in zip(parts, maxes)]
    sums = [jnp.sum(functools.reduce(jnp.add, ps), axis=-1, keepdims=True) for ps in probs]
    gates = [jnp.sum(functools.reduce(jnp.add, pt), axis=-1, keepdims=True) * (1.0 / (MOBA_BLOCK * SCALE))
             for pt in parts]
    m_all, l_all, g_all = m_s[...], l_s[...], g_s[...]
    for bi in range(n_here):
        jb = gi * n_here + bi
        o_s[jb] = _dot_nt(jnp.concatenate(probs[bi], axis=1).astype(MXU_DTYPE), block_pages(vt_refs, bi))
        here = lane == jb
        m_all = jnp.where(here, maxes[bi], m_all)
        l_all = jnp.where(here, sums[bi], l_all)
        g_all = jnp.where(here, gates[bi], g_all)
    m_s[...] = m_all
    l_s[...] = l_all
    g_s[...] = g_all

    @pl.when(gi == pl.num_programs(1) - 1)
    def _():
        pad = jnp.zeros((LANES - t, kw), F32)
        kn = jnp.concatenate([kn_ref[...], pad], axis=0).astype(MXU_DTYPE)
        vn = jnp.concatenate([vn_ref[...], pad], axis=0).astype(MXU_DTYPE)
        sn = _dot_nt(qs, kn)
        sn = jnp.where(_lane_iota(sn.shape) <= (_row_iota(sn.shape) & (t - 1)), sn, NEG)
        m_own = jnp.max(sn, axis=-1, keepdims=True)
        p_own = jnp.exp(sn - m_own)
        l_own = jnp.sum(p_own, axis=-1, keepdims=True)
        o_own = _dot(p_own.astype(MXU_DTYPE), vn)

        nbp = -(-n_blocks // SUBLANES) * SUBLANES
        gate_t = jnp.transpose(jnp.concatenate([g_s[...], jnp.zeros((LANES - rows, LANES), F32)], axis=0))[0:nbp]
        sel_t = _topk_select_rows(gate_t, _row_iota(gate_t.shape) < n_blocks, n_blocks).astype(F32)
        sel_t = jnp.concatenate([sel_t, jnp.zeros((LANES - nbp, LANES), F32)], axis=0)
        sel = jnp.transpose(sel_t)[0:rows] > 0.0
        mall = m_s[...]
        mx = jnp.maximum(jnp.max(jnp.where(sel, mall, NEG), axis=-1, keepdims=True), m_own)
        w = jnp.where(sel, jnp.exp(mall - mx), 0.0)
        w_own = jnp.exp(m_own - mx)
        den = jnp.sum(w * l_s[...], axis=-1, keepdims=True) + w_own * l_own
        num = w_own * o_own
        for jb in range(n_blocks):
            num = num + w[:, jb:jb + 1] * o_s[jb]
        o = num / den
        for kv in range(B_KV_HEADS):
            r0 = kv * B_GROUP * t
            src = o[r0:r0 + B_GROUP * t, (kv // 2) * LANES:(kv // 2 + 1) * LANES]
            o_ref[:, kv * LANES:(kv + 1) * LANES] = _merge_halves(src[0:t], kv % 2, src[t:2 * t], kv % 2)


def _moba_decode_call(q, kn, vn, cache_kt, cache_vt, layer, pt_flat, n_pages, pp):
    n, t, qw = q.shape
    kw, page = cache_kt.shape[2], cache_kt.shape[3]
    per_block = MOBA_BLOCK // page
    past = n_pages * page
    n_blocks = past // MOBA_BLOCK
    assert MOBA_BLOCK % page == 0 and pp % per_block == 0 and n_pages % pp == 0 and past % MOBA_BLOCK == 0
    assert n_blocks <= LANES and t <= LANES and (t & (t - 1)) == 0 and t % SUBLANES == 0 and B_GROUP == 2
    assert page == LANES and kw == B_KV_HEADS * HEAD_DIM
    rows = B_HEADS * t
    assert rows <= LANES
    per_seq = lambda w: pl.BlockSpec((None, t, w), lambda b, g, pt: (b, 0, 0))
    grid_spec = pltpu.PrefetchScalarGridSpec(
        num_scalar_prefetch=1,
        grid=(n, n_pages // pp),
        in_specs=[per_seq(qw), per_seq(kw), per_seq(kw)]
        + _paged_specs(layer, pp, n_pages, (kw, page)) + _paged_specs(layer, pp, n_pages, (kw, page)),
        out_specs=per_seq(qw),
        scratch_shapes=[pltpu.VMEM((rows, kw), F32),
                        pltpu.VMEM((rows, LANES), F32), pltpu.VMEM((rows, LANES), F32),
                        pltpu.VMEM((rows, LANES), F32), pltpu.VMEM((n_blocks, rows, kw), F32)],
    )
    return pl.pallas_call(
        functools.partial(_moba_decode_kernel, pp=pp, n_blocks=n_blocks),
        grid_spec=grid_spec,
        out_shape=jax.ShapeDtypeStruct((n, t, qw), F32),
        compiler_params=_cparams(("arbitrary", "arbitrary"), VMEM_LIMIT),
        name="moba_attn_decode",
    )(pt_flat, q, kn, vn, *([cache_kt] * pp), *([cache_vt] * pp))


G_SPLITS = (
    (0, A_HEADS * 2 * HEAD_DIM, True), (512, A_KV_HEADS * 2 * HEAD_DIM, True), (768, A_KV_HEADS * 2 * HEAD_DIM, False),
    (1024, B_HEADS * HEAD_DIM, True), (1536, B_KV_HEADS * HEAD_DIM, True), (1792, B_KV_HEADS * HEAD_DIM, False),
)
L_SPLITS = ((0, C_HEADS * HEAD_DIM, True), (1024, C_KV_HEADS * HEAD_DIM, True), (1152, C_KV_HEADS * HEAD_DIM, False))


def kernel(x_prompt, x_sample, cache_a_k, cache_a_v, cache_b_k, cache_b_v, state_c_k, state_c_v, state_ffn,
           page_table, norm_mix, w_in_g, w_out_g, diff_lambda, diff_subln, w_in_l, w_out_l, sinks, norm_ffn,
           w_up, conv_w, conv_b, w_down, norm_final):
    bsz, seq, d = x_prompt.shape
    nd, t_dec, _ = x_sample.shape
    depth = norm_mix.shape[0]
    ng, pool, page = cache_a_k.shape[:3]
    n_pages = page_table.shape[1]
    past = n_pages * page
    buf = state_c_k.shape[2]
    dff2 = w_up.shape[2]

    cast = lambda w: w.astype(MXU_DTYPE)
    w_in_g, w_out_g, w_in_l, w_out_l, w_up_c, w_down_c = map(cast, (w_in_g, w_out_g, w_in_l, w_out_l, w_up, w_down))

    kw_a = A_KV_HEADS * 2 * HEAD_DIM
    kw_b = B_KV_HEADS * HEAD_DIM
    cakt = jnp.transpose(cache_a_k, (0, 1, 3, 4, 5, 2)).reshape(ng, pool, kw_a, page)
    cav = cache_a_v.reshape(ng, pool, page * A_KV_HEADS, 2 * HEAD_DIM)
    cbkt = jnp.transpose(cache_b_k, (0, 1, 3, 4, 2)).reshape(ng, pool, kw_b, page)
    cbvt = jnp.transpose(cache_b_v, (0, 1, 3, 4, 2)).reshape(ng, pool, kw_b, page)
    pt_flat = page_table.reshape(-1)

    tab_p = _rope_tables(jnp.arange(seq, dtype=jnp.int32))
    pos_s = past + jnp.repeat(jnp.arange(t_dec, dtype=jnp.int32), nd)
    tab_s = _rope_tables(pos_s)

    n_s = nd * t_dec
    xp = x_prompt.reshape(bsz * seq, d)
    xs = jnp.swapaxes(x_sample, 0, 1).reshape(n_s, d)

    tm_p = _tile(seq, 512)
    tm_s = _tile(n_s, 512)
    tf = 256 if (dff2 // 2) % 256 == 0 else LANES
    halo_p = SUBLANES
    halo_s = (CONV_W - 1) * nd
    pp = 32 if n_pages % 32 == 0 else n_pages
    dec_blk = 8 if nd % 8 == 0 else nd
    w_up_c = jnp.swapaxes(w_up_c.reshape(depth, d, dff2 // tf, tf), 1, 2)
    r3 = lambda a: a.reshape(bsz, seq, -1)
    feature_major = lambda a: jnp.swapaxes(r3(a), 1, 2)

    def to_seq_major(a):
        return jnp.swapaxes(a.reshape(t_dec, nd, -1), 0, 1)

    def to_pos_major(a):
        return jnp.swapaxes(a, 0, 1).reshape(n_s, -1)

    outs = {k: [] for k in ("akp", "avp", "bkp", "bvp", "ckp", "cvp", "fp", "aks", "avs", "bks", "bvs", "cks", "cvs", "fs")}
    op = os_ = None
    w_prev = None
    for l in range(depth):
        i = l // 2
        if l % 2 == 0:
            lam_init = 0.8 - 0.6 * math.exp(-0.3 * l)
            qa, ka, va, qb, kb, vb = _proj_call(xp, norm_mix[l], w_in_g[i], tab_p, G_SPLITS, tm_p)
            oa = _diff_prompt_call(r3(qa), feature_major(ka), r3(va), diff_lambda[i], diff_subln[i], lam_init,
                                   MOBA_BLOCK)
            ob = _moba_prompt_call(r3(qb), feature_major(kb), r3(vb))
            op = jnp.concatenate([oa, ob], axis=-1).reshape(bsz * seq, -1)
            outs["akp"].append(ka.reshape(bsz, seq, A_KV_HEADS, 2, HEAD_DIM))
            outs["avp"].append(va.reshape(bsz, seq, A_KV_HEADS, 2 * HEAD_DIM))
            outs["bkp"].append(kb.reshape(bsz, seq, B_KV_HEADS, HEAD_DIM))
            outs["bvp"].append(vb.reshape(bsz, seq, B_KV_HEADS, HEAD_DIM))

            qa, ka, va, qb, kb, vb = map(to_seq_major, _proj_call(xs, norm_mix[l], w_in_g[i], tab_s, G_SPLITS, tm_s))
            oa = _diff_decode_call(qa, ka, va, cakt, cav, i, pt_flat, n_pages, diff_lambda[i], diff_subln[i],
                                   lam_init, pp)
            ob = _moba_decode_call(qb, kb, vb, cbkt, cbvt, i, pt_flat, n_pages, pp)
            os_ = to_pos_major(jnp.concatenate([oa, ob], axis=-1))
            outs["aks"].append(ka.reshape(nd, t_dec, A_KV_HEADS, 2, HEAD_DIM))
            outs["avs"].append(va.reshape(nd, t_dec, A_KV_HEADS, 2 * HEAD_DIM))
            outs["bks"].append(kb.reshape(nd, t_dec, B_KV_HEADS, HEAD_DIM))
            outs["bvs"].append(vb.reshape(nd, t_dec, B_KV_HEADS, HEAD_DIM))
            w_o = w_out_g[i]
        else:
            q, k, v = _proj_call(xp, norm_mix[l], w_in_l[i], tab_p, L_SPLITS, tm_p)
            op = _local_prompt_call(r3(q), feature_major(k), r3(v), sinks[i])
            op = op.reshape(bsz * seq, -1)
            outs["ckp"].append(r3(k)[:, seq - buf:].reshape(bsz, buf, C_KV_HEADS, HEAD_DIM))
            outs["cvp"].append(r3(v)[:, seq - buf:].reshape(bsz, buf, C_KV_HEADS, HEAD_DIM))

            q, k, v = map(to_seq_major, _proj_call(xs, norm_mix[l], w_in_l[i], tab_s, L_SPLITS, tm_s))
            o, ck, cv = _local_decode_call(q, state_c_k[i].reshape(nd, buf, LANES), state_c_v[i].reshape(nd, buf, LANES),
                                           k, v, sinks[i], dec_blk)
            os_ = to_pos_major(o)
            outs["cks"].append(ck.reshape(nd, buf, C_KV_HEADS, HEAD_DIM))
            outs["cvs"].append(cv.reshape(nd, buf, C_KV_HEADS, HEAD_DIM))
            w_o = w_out_l[i]

        final = l == depth - 1
        prev_p = jnp.zeros((bsz, halo_p, dff2), F32)
        xp, tail_p = _ffn_call(xp, op, w_o, norm_ffn[l], prev_p, w_up_c[l], conv_w[l], conv_b[l], w_down_c[l],
                               norm_final, seq_rows=seq, halo=halo_p, shift=1, final_norm=final, tm=tm_p, tf=tf)
        outs["fp"].append(tail_p[:, halo_p - (CONV_W - 1):])
        prev_s = jnp.swapaxes(state_ffn[l], 0, 1).reshape(1, halo_s, dff2)
        xs, tail_s = _ffn_call(xs, os_, w_o, norm_ffn[l], prev_s, w_up_c[l], conv_w[l], conv_b[l], w_down_c[l],
                               norm_final, seq_rows=n_s, halo=halo_s, shift=nd, final_norm=final, tm=tm_s, tf=tf)
        outs["fs"].append(jnp.swapaxes(tail_s.reshape(CONV_W - 1, nd, dff2), 0, 1))

    st = lambda key: jnp.stack(outs[key])
    y_p = xp.reshape(bsz, seq, d)
    y_s = jnp.swapaxes(xs.reshape(t_dec, nd, d), 0, 1)
    return (y_p, y_s, st("akp"), st("avp"), st("bkp"), st("bvp"), st("ckp"), st("cvp"), st("fp"),
            st("aks"), st("avs"), st("bks"), st("bvs"), st("cks"), st("cvs"), st("fs"))
```

```python
import functools
import math

import jax
import jax.numpy as jnp
from jax import lax
from jax.experimental import pallas as pl
from jax.experimental.pallas import tpu as pltpu

HEAD_DIM = 64
SCALE = HEAD_DIM ** -0.5
ROT_DIM = HEAD_DIM // 4
ROPE_THETA = 500000.0
NORM_EPS = 1e-6
SUBLN_EPS = 1e-5
A_HEADS, A_KV_HEADS = 4, 2
A_GROUP = A_HEADS // A_KV_HEADS
B_HEADS, B_KV_HEADS = 8, 4
B_GROUP = B_HEADS // B_KV_HEADS
MOBA_BLOCK = 256
MOBA_TOPK = 3
C_HEADS, C_KV_HEADS = 16, 2
C_GROUP = C_HEADS // C_KV_HEADS
WINDOW = 128
CONV_W = 3

LANES = 128
SUBLANES = 8
HALF = LANES // 2
NEG = -1e30
ROW_CHUNK = 64
MXU_DTYPE = jnp.bfloat16
VMEM_LIMIT = 52 * 1024 * 1024

F32 = jnp.float32


def _cparams(sem, vmem=None):
    return pltpu.CompilerParams(dimension_semantics=sem, vmem_limit_bytes=vmem)


def _tile(n, pref):
    t = min(n, pref)
    while n % t or t % SUBLANES:
        t -= 1
    return t


def _lane_iota(shape):
    return lax.broadcasted_iota(jnp.int32, shape, len(shape) - 1)


def _row_iota(shape):
    return lax.broadcasted_iota(jnp.int32, shape, len(shape) - 2)


def _half_place(chunk, src_half, dst_half):
    x = chunk if src_half == dst_half else pltpu.roll(chunk, HALF, axis=1)
    lane = _lane_iota(x.shape)
    keep = (lane >= HALF) if dst_half else (lane < HALF)
    return jnp.where(keep, x, 0.0)


def _merge_halves(lo, lo_src_half, hi, hi_src_half):
    a = lo if lo_src_half == 0 else pltpu.roll(lo, HALF, axis=1)
    b = hi if hi_src_half == 1 else pltpu.roll(hi, HALF, axis=1)
    return jnp.where(_lane_iota(a.shape) < HALF, a, b)


def _dot_nt(a, b):
    return lax.dot_general(a, b, (((1,), (1,)), ((), ())), preferred_element_type=F32)


def _dot(a, b):
    return jnp.dot(a, b, preferred_element_type=F32)


def _rms(x, g, eps):
    ms = jnp.mean(x * x, axis=-1, keepdims=True)
    return x * lax.rsqrt(ms + eps) * g


def _gelu(x):
    return 0.5 * x * (1.0 + lax.erf(x * math.sqrt(0.5)))


def _diff_lambda(lp, lam_init):
    a = jnp.sum(lp[0:1] * lp[1:2], axis=-1, keepdims=True)
    b = jnp.sum(lp[2:3] * lp[3:4], axis=-1, keepdims=True)
    return jnp.exp(a) - jnp.exp(b) + lam_init


def _rope_tables(pos):
    half = ROT_DIM // 2
    inv = ROPE_THETA ** (-jnp.arange(half, dtype=F32) * 2.0 / ROT_DIM)
    ang = pos.astype(F32)[:, None] * inv[None, :]
    cos, sin = jnp.cos(ang), jnp.sin(ang)
    t = pos.shape[0]
    rest = HEAD_DIM - ROT_DIM
    zh = jnp.zeros((t, half), F32)
    c = jnp.concatenate([cos, cos, jnp.ones((t, rest), F32)], axis=-1)
    sn = jnp.concatenate([-sin, zh, jnp.zeros((t, rest), F32)], axis=-1)
    sp = jnp.concatenate([zh, sin, jnp.zeros((t, rest), F32)], axis=-1)
    rep = LANES // HEAD_DIM
    return tuple(jnp.tile(a, (1, rep)) for a in (c, sn, sp)) + (cos.T, sin.T)


ROWS, COLS, BOTH = "rows", "cols", "both"


def _proj_kernel(x_ref, g_ref, w_ref, cos_ref, sn_ref, sp_ref, cost_ref, sint_ref, *out_refs, splits):
    xn = _rms(x_ref[...], g_ref[...], NORM_EPS).astype(MXU_DTYPE)
    half = ROT_DIM // 2
    outs = iter(out_refs)
    for start, width, rotary, form in splits:
        y = _dot(xn, w_ref[:, start:start + width])
        if form in (ROWS, BOTH):
            o_ref = next(outs)
            cos, sn, sp = cos_ref[...], sn_ref[...], sp_ref[...]
            for c in range(width // LANES):
                yc = y[:, c * LANES:(c + 1) * LANES]
                if rotary:
                    yc = (yc * cos + pltpu.roll(yc, LANES - half, axis=1) * sn
                          + pltpu.roll(yc, half, axis=1) * sp)
                o_ref[:, c * LANES:(c + 1) * LANES] = yc
        if form in (COLS, BOTH):
            o_ref = next(outs)
            yt = jnp.transpose(y)
            if not rotary:
                o_ref[...] = yt
                continue
            cos_t, sin_t = cost_ref[...], sint_ref[...]
            for h in range(width // HEAD_DIM):
                r = h * HEAD_DIM
                x1, x2 = yt[r:r + half], yt[r + half:r + 2 * half]
                o_ref[r:r + half] = x1 * cos_t - x2 * sin_t
                o_ref[r + half:r + 2 * half] = x2 * cos_t + x1 * sin_t
                o_ref[r + 2 * half:r + HEAD_DIM] = yt[r + 2 * half:r + HEAD_DIM]


def _proj_call(x, g, w, tables, splits, tm, seq_rows):
    n, d = x.shape
    nt = tables[0].shape[0]
    assert n % tm == 0 and nt % tm == 0 and seq_rows % tm == 0 and ROT_DIM // 2 == SUBLANES
    ntb = nt // tm
    tps = seq_rows // tm
    row = lambda i: (i, 0)
    fixed = lambda i: (0, 0)
    in_specs = [pl.BlockSpec((tm, d), row), pl.BlockSpec((1, d), fixed), pl.BlockSpec(w.shape, fixed)]
    in_specs += [pl.BlockSpec((tm, LANES), lambda i: (i % ntb, 0))] * 3
    in_specs += [pl.BlockSpec((SUBLANES, tm), lambda i: (0, i % ntb))] * 2
    out_specs, out_shape = [], []
    for _, wd, _, form in splits:
        if form in (ROWS, BOTH):
            out_specs.append(pl.BlockSpec((tm, wd), row))
            out_shape.append(jax.ShapeDtypeStruct((n, wd), F32))
        if form in (COLS, BOTH):
            out_specs.append(pl.BlockSpec((None, wd, tm), lambda i: (i // tps, 0, i % tps)))
            out_shape.append(jax.ShapeDtypeStruct((n // seq_rows, wd, seq_rows), F32))
    return pl.pallas_call(
        functools.partial(_proj_kernel, splits=splits),
        grid=(n // tm,),
        in_specs=in_specs,
        out_specs=out_specs,
        out_shape=out_shape,
        compiler_params=_cparams(("arbitrary",), VMEM_LIMIT),
        name="norm_proj_rope",
    )(x, g.reshape(1, d), w, *tables)


def _ffn_kernel(x_ref, o1_ref, o2_ref, wo1_ref, wo2_ref, g_ref, pa_ref, pb_ref, wa_ref, wb_ref, cwa_ref, cwb_ref,
                cba_ref, cbb_ref, wd_ref, gf_ref, xo_ref, sa_ref, sb_ref,
                x1_s, hn_s, acc_s, ua_s, ub_s, ha_s, hb_s, *, tiles_per_seq, halo, shift, final_norm):
    i, j = pl.program_id(0), pl.program_id(1)
    nj = pl.num_programs(1)
    tm = x_ref.shape[0]

    @pl.when(j == 0)
    def _():
        x1 = (x_ref[...] + _dot(o1_ref[...].astype(MXU_DTYPE), wo1_ref[...])
              + _dot(o2_ref[...].astype(MXU_DTYPE), wo2_ref[...]))
        x1_s[...] = x1
        hn_s[...] = _rms(x1, g_ref[...], NORM_EPS).astype(MXU_DTYPE)
        acc_s[...] = jnp.zeros_like(acc_s)

    first = (i % tiles_per_seq) == 0

    @pl.when(first)
    def _():
        ua_s[0:halo] = pa_ref[...]
        ub_s[0:halo] = pb_ref[...]

    @pl.when(jnp.logical_not(first))
    def _():
        ua_s[0:halo] = ha_s[j]
        ub_s[0:halo] = hb_s[j]

    hn = hn_s[...]
    ua_s[halo:halo + tm] = _dot(hn, wa_ref[...])
    ub_s[halo:halo + tm] = _dot(hn, wb_ref[...])

    def conv(u_s, cw_ref, cb_ref):
        c = cb_ref[...]
        for tap in range(CONV_W):
            off = halo - (CONV_W - 1 - tap) * shift
            c = c + u_s[off:off + tm] * cw_ref[tap:tap + 1]
        return c

    a = conv(ua_s, cwa_ref, cba_ref)
    b = conv(ub_s, cwb_ref, cbb_ref)
    act = (_gelu(a) * b).astype(MXU_DTYPE)
    acc_s[...] += _dot(act, wd_ref[...])

    tail_a = ua_s[tm:tm + halo]
    tail_b = ub_s[tm:tm + halo]
    ha_s[j] = tail_a
    hb_s[j] = tail_b
    sa_ref[...] = tail_a
    sb_ref[...] = tail_b

    @pl.when(j == nj - 1)
    def _():
        xo = x1_s[...] + acc_s[...]
        if final_norm:
            xo = _rms(xo, gf_ref[...], NORM_EPS)
        xo_ref[...] = xo


def _ffn_call(x, o_parts, w_out, g, prev, w_up, conv_w, conv_b, w_down, g_final, *, seq_rows, halo, shift,
              final_norm, tm, tf):
    n, d = x.shape
    dff = w_down.shape[0]
    half = w_out.shape[0] // 2
    assert seq_rows % tm == 0 and dff % tf == 0 and tm >= halo
    tiles_per_seq = seq_rows // tm
    nj = dff // tf
    row = lambda i, j: (i, 0)
    fixed = lambda i, j: (0, 0)
    (o1, c1), (o2, c2) = o_parts
    in_specs = [
        pl.BlockSpec((tm, d), row),
        pl.BlockSpec((tm, half), lambda i, j: (i, c1)), pl.BlockSpec((tm, half), lambda i, j: (i, c2)),
        pl.BlockSpec((half, d), fixed), pl.BlockSpec((half, d), lambda i, j: (1, 0)), pl.BlockSpec((1, d), fixed),
        pl.BlockSpec((None, halo, tf), lambda i, j: (i // tiles_per_seq, 0, j)),
        pl.BlockSpec((None, halo, tf), lambda i, j: (i // tiles_per_seq, 0, nj + j)),
        pl.BlockSpec((d, tf), lambda i, j: (0, j)), pl.BlockSpec((d, tf), lambda i, j: (0, nj + j)),
        pl.BlockSpec((CONV_W, tf), lambda i, j: (0, j)), pl.BlockSpec((CONV_W, tf), lambda i, j: (0, nj + j)),
        pl.BlockSpec((1, tf), lambda i, j: (0, j)), pl.BlockSpec((1, tf), lambda i, j: (0, nj + j)),
        pl.BlockSpec((tf, d), lambda i, j: (j, 0)), pl.BlockSpec((1, d), fixed),
    ]
    tail_spec = pl.BlockSpec((None, halo, tf), lambda i, j: (i, 0, j))
    xo, ta, tb = pl.pallas_call(
        functools.partial(_ffn_kernel, tiles_per_seq=tiles_per_seq, halo=halo, shift=shift,
                          final_norm=final_norm),
        grid=(n // tm, nj),
        in_specs=in_specs,
        out_specs=[pl.BlockSpec((tm, d), row), tail_spec, tail_spec],
        out_shape=[jax.ShapeDtypeStruct((n, d), F32), jax.ShapeDtypeStruct((n // tm, halo, dff), F32),
                   jax.ShapeDtypeStruct((n // tm, halo, dff), F32)],
        scratch_shapes=[
            pltpu.VMEM((tm, d), F32), pltpu.VMEM((tm, d), MXU_DTYPE), pltpu.VMEM((tm, d), F32),
            pltpu.VMEM((halo + tm, tf), F32), pltpu.VMEM((halo + tm, tf), F32),
            pltpu.VMEM((nj, halo, tf), F32), pltpu.VMEM((nj, halo, tf), F32),
        ],
        compiler_params=_cparams(("arbitrary", "arbitrary"), VMEM_LIMIT),
        name="outproj_convffn",
    )(x, o1, o2, w_out, w_out, g.reshape(1, d), prev, prev, w_up, w_up, conv_w, conv_w,
      conv_b.reshape(1, -1), conv_b.reshape(1, -1), w_down, g_final.reshape(1, d))
    last = slice(tiles_per_seq - 1, None, tiles_per_seq)
    return xo, jnp.concatenate([ta[last], tb[last]], axis=-1)


def _flash_update(qs_s, kt, v, m_s, l_s, acc_s, mask_fn):
    tk = kt.shape[1]
    for r0 in range(0, qs_s.shape[0], ROW_CHUNK):
        rs = slice(r0, r0 + ROW_CHUNK)
        s = _dot(qs_s[rs], kt)
        if mask_fn is not None:
            s = mask_fn(r0, s)
        parts = [s[:, i * LANES:(i + 1) * LANES] for i in range(tk // LANES)]
        mx = jnp.max(functools.reduce(jnp.maximum, parts), axis=-1, keepdims=True)
        m_prev = m_s[rs]
        m_new = jnp.maximum(m_prev, mx)
        alpha = jnp.exp(m_prev - m_new)
        ps = [jnp.exp(p - m_new) for p in parts]
        l_s[rs] = alpha * l_s[rs] + jnp.sum(functools.reduce(jnp.add, ps), axis=-1, keepdims=True)
        acc_s[rs] = alpha * acc_s[rs] + _dot(jnp.concatenate(ps, axis=1).astype(MXU_DTYPE), v)
        m_s[rs] = m_new


def _causal_mask(tq):
    def mask(r0, s):
        r = (_row_iota(s.shape) + r0) & (tq - 1)
        return jnp.where(_lane_iota(s.shape) <= r, s, NEG)
    return mask


def _diff_finish(acc, l, lam, g, lam_init, rows):
    o = acc[0:rows] / l[0:rows] - lam * (acc[rows:2 * rows] / l[rows:2 * rows])
    return _rms(o, g, SUBLN_EPS) * (1.0 - lam_init)


def _diff_prompt_kernel(q_ref, kt_ref, v_ref, lam_ref, g_ref, o_ref, qs_s, m_s, l_s, acc_s, *, lam_init):
    qi = pl.program_id(2)
    tq = q_ref.shape[0]
    rows = A_GROUP * tq
    q = q_ref[...]
    lane = _lane_iota((tq, LANES))
    for m in range(2):
        for g in range(A_GROUP):
            chunk = q[:, g * LANES:(g + 1) * LANES]
            keep = (lane >= HALF) if m else (lane < HALF)
            r0 = (m * A_GROUP + g) * tq
            qs_s[r0:r0 + tq] = (jnp.where(keep, chunk, 0.0) * SCALE).astype(MXU_DTYPE)
    m_s[...] = jnp.full_like(m_s, NEG)
    l_s[...] = jnp.zeros_like(l_s)
    acc_s[...] = jnp.zeros_like(acc_s)

    def step(j, mask_fn):
        start = pl.multiple_of(j * tq, tq)
        kt = kt_ref[:, pl.ds(start, tq)].astype(MXU_DTYPE)
        v = v_ref[pl.ds(start, tq), :].astype(MXU_DTYPE)
        _flash_update(qs_s, kt, v, m_s, l_s, acc_s, mask_fn)

    step(qi, _causal_mask(tq))
    lax.fori_loop(0, qi, lambda j, c: (step(j, None), c)[1], 0)

    lam = _diff_lambda(lam_ref[...], lam_init)
    o = _diff_finish(acc_s[...], l_s[...], lam, g_ref[...], lam_init, rows)
    for g in range(A_GROUP):
        o_ref[:, g * LANES:(g + 1) * LANES] = o[g * tq:(g + 1) * tq]


def _diff_prompt_call(qa, kat, va, lam_p, subln_g, lam_init, tq):
    b, s, _ = qa.shape
    assert s % tq == 0 and (tq & (tq - 1)) == 0 and (2 * A_GROUP * tq) % ROW_CHUNK == 0
    kvw = 2 * HEAD_DIM
    qw = A_GROUP * kvw
    rows = 2 * A_GROUP * tq
    return pl.pallas_call(
        functools.partial(_diff_prompt_kernel, lam_init=lam_init),
        grid=(b, A_KV_HEADS, s // tq),
        in_specs=[
            pl.BlockSpec((None, tq, qw), lambda bi, h, i: (bi, i, h)),
            pl.BlockSpec((None, kvw, s), lambda bi, h, i: (bi, h, 0)),
            pl.BlockSpec((None, s, kvw), lambda bi, h, i: (bi, 0, h)),
            pl.BlockSpec(lam_p.shape, lambda bi, h, i: (0, 0)),
            pl.BlockSpec((1, kvw), lambda bi, h, i: (0, 0)),
        ],
        out_specs=pl.BlockSpec((None, tq, qw), lambda bi, h, i: (bi, i, h)),
        out_shape=jax.ShapeDtypeStruct((b, s, A_KV_HEADS * qw), F32),
        scratch_shapes=[pltpu.VMEM((rows, LANES), MXU_DTYPE), pltpu.VMEM((rows, LANES), F32),
                        pltpu.VMEM((rows, LANES), F32), pltpu.VMEM((rows, LANES), F32)],
        compiler_params=_cparams(("arbitrary", "arbitrary", "arbitrary"), VMEM_LIMIT),
        name="diff_attn_prompt",
    )(qa, kat, va, lam_p, subln_g.reshape(1, kvw))


def _topk_select_rows(gate_t, valid, n):
    gm = jnp.where(valid, gate_t, -jnp.inf)
    idx = _row_iota(gm.shape)
    rank = jnp.zeros(gm.shape, F32)
    for i in range(n):
        gi = gm[i:i + 1, :]
        beats = (gi > gm) | ((gi == gm) & (idx > i))
        rank = rank + beats.astype(F32)
    return valid & (rank < MOBA_TOPK)


def _moba_prompt_kernel(q_ref, kt_ref, v_ref, o_ref, qs_s, qf_s, km_s, sel_s, m_s, l_s, acc_s, *, nb):
    qi = pl.program_id(2)
    tq = q_ref.shape[0]
    nbp = -(-nb // SUBLANES) * SUBLANES
    q = q_ref[...]
    for kvp in range(2):
        for g in range(B_GROUP):
            piece = _half_place(q[:, kvp * LANES:(kvp + 1) * LANES], g, kvp)
            r0 = (kvp * B_GROUP + g) * tq
            qf_s[r0:r0 + tq] = piece
            qs_s[r0:r0 + tq] = (piece * SCALE).astype(MXU_DTYPE)

    @pl.when(qi == 0)
    def _():
        col_block = _lane_iota((LANES, LANES)) & (nb - 1)
        km = jnp.zeros((LANES, LANES), F32)
        for n in range(nb):
            mean_n = jnp.sum(kt_ref[:, n * MOBA_BLOCK:(n + 1) * MOBA_BLOCK], axis=-1, keepdims=True) * (1.0 / MOBA_BLOCK)
            km = jnp.where(col_block == n, mean_n, km)
        km_s[...] = km

    m_s[...] = jnp.full_like(m_s, NEG)
    l_s[...] = jnp.zeros_like(l_s)
    acc_s[...] = jnp.zeros_like(acc_s)

    gate = jnp.dot(qf_s[...], km_s[...], precision=lax.Precision.HIGHEST, preferred_element_type=F32)
    gate_t = jnp.transpose(gate)[0:nbp]
    sel_t = _topk_select_rows(gate_t, _row_iota(gate_t.shape) < jnp.minimum(qi, nb), nb).astype(F32)
    sel_t = jnp.concatenate([sel_t, jnp.zeros((LANES - nbp, gate.shape[0]), F32)], axis=0)
    sel_s[...] = jnp.transpose(sel_t)

    def block_step(back, carry):
        j = qi - back
        start = pl.multiple_of(j * tq, tq)
        kt = kt_ref[:, pl.ds(start, tq)].astype(MXU_DTYPE)
        v = v_ref[pl.ds(start, tq), :].astype(MXU_DTYPE)

        def mask(r0, s):
            sel = sel_s[r0:r0 + ROW_CHUNK]
            picked = jnp.sum(jnp.where(_lane_iota(sel.shape) == j, sel, 0.0), axis=-1, keepdims=True)
            own_limit = (_row_iota((ROW_CHUNK, 1)) + r0) & (tq - 1)
            limit = jnp.where(back == 0, own_limit, jnp.where(picked > 0.0, tq, -1))
            return jnp.where(_lane_iota(s.shape) <= limit, s, NEG)

        _flash_update(qs_s, kt, v, m_s, l_s, acc_s, mask)
        return carry

    lax.fori_loop(0, qi + 1, block_step, 0)

    o = acc_s[...] / l_s[...]
    for kvp in range(2):
        r0 = kvp * B_GROUP * tq
        o_ref[:, kvp * LANES:(kvp + 1) * LANES] = _merge_halves(o[r0:r0 + tq], kvp, o[r0 + tq:r0 + 2 * tq], kvp)


def _moba_prompt_call(qb, kbt, vb):
    b, s, _ = qb.shape
    tq = MOBA_BLOCK
    nb = s // tq
    assert s % tq == 0 and LANES % nb == 0 and B_GROUP == 2
    pairs = B_KV_HEADS // 2
    qw = 2 * B_GROUP * HEAD_DIM
    rows = 2 * B_GROUP * tq
    assert rows % ROW_CHUNK == 0
    return pl.pallas_call(
        functools.partial(_moba_prompt_kernel, nb=nb),
        grid=(b, pairs, nb),
        in_specs=[
            pl.BlockSpec((None, tq, qw), lambda bi, h, i: (bi, i, h)),
            pl.BlockSpec((None, LANES, s), lambda bi, h, i: (bi, h, 0)),
            pl.BlockSpec((None, s, LANES), lambda bi, h, i: (bi, 0, h)),
        ],
        out_specs=pl.BlockSpec((None, tq, qw), lambda bi, h, i: (bi, i, h)),
        out_shape=jax.ShapeDtypeStruct((b, s, pairs * qw), F32),
        scratch_shapes=[pltpu.VMEM((rows, LANES), MXU_DTYPE), pltpu.VMEM((rows, LANES), F32),
                        pltpu.VMEM((LANES, LANES), F32), pltpu.VMEM((rows, LANES), F32),
                        pltpu.VMEM((rows, LANES), F32), pltpu.VMEM((rows, LANES), F32),
                        pltpu.VMEM((rows, LANES), F32)],
        compiler_params=_cparams(("arbitrary", "arbitrary", "arbitrary"), VMEM_LIMIT),
        name="moba_attn_prompt",
    )(qb, kbt, vb)


def _window_q(q, rows):
    pieces = []
    for h in range(C_HEADS):
        chunk = q[:, (h // 2) * LANES:(h // 2 + 1) * LANES]
        pieces.append(_half_place(chunk, h % 2, h // C_GROUP))
    return (jnp.concatenate(pieces, axis=0) * SCALE).astype(MXU_DTYPE)


def _sink_column(sink_ref, rows):
    return jnp.concatenate([jnp.full((rows, 1), sink_ref[h], F32) for h in range(C_HEADS)], axis=0)


def _window_out(o, rows):
    chunks = []
    for c in range(C_HEADS // 2):
        kv = (2 * c) // C_GROUP
        lo = o[(2 * c) * rows:(2 * c + 1) * rows]
        hi = o[(2 * c + 1) * rows:(2 * c + 2) * rows]
        chunks.append(_merge_halves(lo, kv, hi, kv))
    return chunks


def _sink_softmax_pv(s, sink, v):
    m = jnp.maximum(jnp.max(s, axis=-1, keepdims=True), sink)
    p = jnp.exp(s - m)
    den = jnp.sum(p, axis=-1, keepdims=True) + jnp.exp(sink - m)
    return _dot(p.astype(MXU_DTYPE), v) / den


def _local_prompt_kernel(sink_ref, q_ref, ktp_ref, ktc_ref, vp_ref, vc_ref, o_ref):
    has_prev = pl.program_id(1) > 0
    kt = jnp.concatenate([ktp_ref[...], ktc_ref[...]], axis=1).astype(MXU_DTYPE)
    v = jnp.concatenate([vp_ref[...], vc_ref[...]], axis=0).astype(MXU_DTYPE)
    col = _lane_iota((WINDOW, 2 * WINDOW))
    rel = WINDOW + _row_iota(col.shape) - col
    visible = (rel >= 0) & (rel < WINDOW) & ((col >= WINDOW) | has_prev)
    for c in range(C_HEADS // 2):
        kv = (2 * c) // C_GROUP
        chunk = q_ref[:, c * LANES:(c + 1) * LANES]
        outs = []
        for hh in range(2):
            sink = sink_ref[2 * c + hh]
            qh = (_half_place(chunk, hh, kv) * SCALE).astype(MXU_DTYPE)
            s = jnp.where(visible, _dot(qh, kt), NEG)
            parts = [s[:, i * LANES:(i + 1) * LANES] for i in range(2 * WINDOW // LANES)]
            m = jnp.maximum(jnp.max(functools.reduce(jnp.maximum, parts), axis=-1, keepdims=True), sink)
            ps = [jnp.exp(p - m) for p in parts]
            den = jnp.sum(functools.reduce(jnp.add, ps), axis=-1, keepdims=True) + jnp.exp(sink - m)
            outs.append(_dot(jnp.concatenate(ps, axis=1).astype(MXU_DTYPE), v) / den)
        o_ref[:, c * LANES:(c + 1) * LANES] = _merge_halves(outs[0], kv, outs[1], kv)


def _local_prompt_call(q, kt, v, sinks):
    b, s, qw = q.shape
    assert s % WINDOW == 0 and C_KV_HEADS * HEAD_DIM == LANES and WINDOW == LANES
    prev = lambda i: jnp.maximum(i - 1, 0)
    return pl.pallas_call(
        _local_prompt_kernel,
        grid=(b, s // WINDOW),
        in_specs=[
            pl.BlockSpec(memory_space=pltpu.SMEM),
            pl.BlockSpec((None, WINDOW, qw), lambda bi, i: (bi, i, 0)),
            pl.BlockSpec((None, LANES, WINDOW), lambda bi, i: (bi, 0, prev(i))),
            pl.BlockSpec((None, LANES, WINDOW), lambda bi, i: (bi, 0, i)),
            pl.BlockSpec((None, WINDOW, LANES), lambda bi, i: (bi, prev(i), 0)),
            pl.BlockSpec((None, WINDOW, LANES), lambda bi, i: (bi, i, 0)),
        ],
        out_specs=pl.BlockSpec((None, WINDOW, qw), lambda bi, i: (bi, i, 0)),
        out_shape=jax.ShapeDtypeStruct((b, s, qw), F32),
        compiler_params=_cparams(("arbitrary", "arbitrary"), VMEM_LIMIT),
        name="window_attn_prompt",
    )(sinks, q, kt, kt, v, v)


def _local_decode_kernel(sink_ref, q_ref, ck_ref, cv_ref, kn_ref, vn_ref, o_ref, ko_ref, vo_ref):
    nseq, t, _ = q_ref.shape
    buf = ck_ref.shape[1]
    sink = _sink_column(sink_ref, t)
    pad = jnp.zeros((2 * WINDOW - buf - t, LANES), F32)
    for sq in range(nseq):
        kn, vn = kn_ref[sq], vn_ref[sq]
        k = jnp.concatenate([ck_ref[sq], kn, pad], axis=0).astype(MXU_DTYPE)
        v = jnp.concatenate([cv_ref[sq], vn, pad], axis=0).astype(MXU_DTYPE)
        s = _dot_nt(_window_q(q_ref[sq], t), k)
        col = _lane_iota(s.shape)
        rel = buf + (_row_iota(s.shape) & (t - 1)) - col
        s = jnp.where((rel >= 0) & (rel < WINDOW) & (col < buf + t), s, NEG)
        o = _sink_softmax_pv(s, sink, v)
        for c, chunk in enumerate(_window_out(o, t)):
            o_ref[sq, :, c * LANES:(c + 1) * LANES] = chunk
        ko_ref[sq, 0:buf - t] = ck_ref[sq, t:buf]
        ko_ref[sq, buf - t:buf] = kn
        vo_ref[sq, 0:buf - t] = cv_ref[sq, t:buf]
        vo_ref[sq, buf - t:buf] = vn


def _local_decode_call(q, ck, cv, kn, vn, sinks, nseq):
    n, t, qw = q.shape
    buf = ck.shape[1]
    assert n % nseq == 0 and buf + t <= 2 * WINDOW and (t & (t - 1)) == 0 and t % SUBLANES == 0
    blk = lambda shape: pl.BlockSpec((nseq,) + shape, lambda i: (i, 0, 0))
    return pl.pallas_call(
        _local_decode_kernel,
        grid=(n // nseq,),
        in_specs=[pl.BlockSpec(memory_space=pltpu.SMEM), blk((t, qw)), blk((buf, LANES)), blk((buf, LANES)),
                  blk((t, LANES)), blk((t, LANES))],
        out_specs=[blk((t, qw)), blk((buf, LANES)), blk((buf, LANES))],
        out_shape=[jax.ShapeDtypeStruct((n, t, qw), F32), jax.ShapeDtypeStruct((n, buf, LANES), F32),
                   jax.ShapeDtypeStruct((n, buf, LANES), F32)],
        compiler_params=_cparams(("arbitrary",), VMEM_LIMIT),
        name="window_attn_decode",
    )(sinks, q, ck, cv, kn, vn)


def _diff_decode_kernel(pt_ref, q_ref, kn_ref, vn_ref, lam_ref, g_ref, *refs, pp, lam_init):
    kt_refs, v_refs = refs[:pp], refs[pp:2 * pp]
    o_ref = refs[2 * pp]
    qs_s, m_s, l_s, acc_s = refs[2 * pp + 1:]
    gi = pl.program_id(1)
    t = q_ref.shape[0]
    kw, page = kt_refs[0].shape
    gr = 2 * A_GROUP * t

    @pl.when(gi == 0)
    def _():
        q = q_ref[...]
        lane = _lane_iota((t, LANES))
        zero = jnp.zeros((t, LANES), F32)
        for kv in range(A_KV_HEADS):
            for m in range(2):
                for g in range(A_GROUP):
                    c = kv * A_GROUP + g
                    keep = (lane >= HALF) if m else (lane < HALF)
                    piece = jnp.where(keep, q[:, c * LANES:(c + 1) * LANES], 0.0) * SCALE
                    full = jnp.concatenate([piece if kk == kv else zero for kk in range(A_KV_HEADS)], axis=1)
                    r0 = ((kv * 2 + m) * A_GROUP + g) * t
                    qs_s[r0:r0 + t] = full
        m_s[...] = jnp.full_like(m_s, NEG)
        l_s[...] = jnp.zeros_like(l_s)
        acc_s[...] = jnp.zeros_like(acc_s)

    def update(s, values_of_head):
        parts = [s[:, i * LANES:(i + 1) * LANES] for i in range(s.shape[1] // LANES)]
        mx = jnp.max(functools.reduce(jnp.maximum, parts), axis=-1, keepdims=True)
        m_prev = m_s[...]
        m_new = jnp.maximum(m_prev, mx)
        alpha = jnp.exp(m_prev - m_new)
        ps = [jnp.exp(p - m_new) for p in parts]
        l_s[...] = alpha * l_s[...] + jnp.sum(functools.reduce(jnp.add, ps), axis=-1, keepdims=True)
        p = jnp.concatenate(ps, axis=1).astype(MXU_DTYPE)
        pv = jnp.concatenate([_dot(p[kv * gr:(kv + 1) * gr], values_of_head(kv)) for kv in range(A_KV_HEADS)], axis=0)
        acc_s[...] = alpha * acc_s[...] + pv
        m_s[...] = m_new

    def paged_values(kv):
        return jnp.concatenate([vr[pl.ds(kv, page, stride=A_KV_HEADS), :].astype(MXU_DTYPE) for vr in v_refs], axis=0)

    qs = qs_s[...].astype(MXU_DTYPE)
    kt = jnp.concatenate([kr[...].astype(MXU_DTYPE) for kr in kt_refs], axis=1)
    update(_dot(qs, kt), paged_values)

    @pl.when(gi == pl.num_programs(1) - 1)
    def _():
        pad = jnp.zeros((LANES - t, kw), F32)
        kn = jnp.concatenate([kn_ref[...], pad], axis=0).astype(MXU_DTYPE)
        vn = jnp.concatenate([vn_ref[...], pad], axis=0).astype(MXU_DTYPE)
        sn = _dot_nt(qs, kn)
        sn = jnp.where(_lane_iota(sn.shape) <= (_row_iota(sn.shape) & (t - 1)), sn, NEG)
        update(sn, lambda kv: vn[:, kv * LANES:(kv + 1) * LANES])
        lam = _diff_lambda(lam_ref[...], lam_init)
        acc, l = acc_s[...], l_s[...]
        for kv in range(A_KV_HEADS):
            o = _diff_finish(acc[kv * gr:(kv + 1) * gr], l[kv * gr:(kv + 1) * gr], lam, g_ref[...], lam_init,
                             A_GROUP * t)
            for g in range(A_GROUP):
                c = kv * A_GROUP + g
                o_ref[:, c * LANES:(c + 1) * LANES] = o[g * t:(g + 1) * t]


def _paged_specs(layer, pp, n_pages, shape):
    def make(i):
        return pl.BlockSpec((None, None) + shape, lambda b, g, pt: (layer, pt[b * n_pages + g * pp + i], 0, 0))
    return [make(i) for i in range(pp)]


def _diff_decode_call(q, kn, vn, cache_kt, cache_v, layer, pt_flat, n_pages, lam_p, subln_g, lam_init, pp):
    n, t, qw = q.shape
    kw, page = cache_kt.shape[2], cache_kt.shape[3]
    assert n_pages % pp == 0 and (t & (t - 1)) == 0 and t % SUBLANES == 0 and t <= LANES and page == LANES
    assert cache_v.shape[2:] == (page * A_KV_HEADS, 2 * HEAD_DIM)
    rows = A_KV_HEADS * 2 * A_GROUP * t
    per_seq = lambda w: pl.BlockSpec((None, t, w), lambda b, g, pt: (b, 0, 0))
    fixed = lambda shape: pl.BlockSpec(shape, lambda b, g, pt: (0, 0))
    grid_spec = pltpu.PrefetchScalarGridSpec(
        num_scalar_prefetch=1,
        grid=(n, n_pages // pp),
        in_specs=[per_seq(qw), per_seq(kw), per_seq(kw), fixed(lam_p.shape), fixed((1, LANES))]
        + _paged_specs(layer, pp, n_pages, (kw, page))
        + _paged_specs(layer, pp, n_pages, (page * A_KV_HEADS, 2 * HEAD_DIM)),
        out_specs=per_seq(qw),
        scratch_shapes=[pltpu.VMEM((rows, kw), F32), pltpu.VMEM((rows, LANES), F32),
                        pltpu.VMEM((rows, LANES), F32), pltpu.VMEM((rows, LANES), F32)],
    )
    return pl.pallas_call(
        functools.partial(_diff_decode_kernel, pp=pp, lam_init=lam_init),
        grid_spec=grid_spec,
        out_shape=jax.ShapeDtypeStruct((n, t, qw), F32),
        compiler_params=_cparams(("arbitrary", "arbitrary"), VMEM_LIMIT),
        name="diff_attn_decode",
    )(pt_flat, q, kn, vn, lam_p, subln_g.reshape(1, LANES), *([cache_kt] * pp), *([cache_v] * pp))


def _moba_decode_kernel(pt_ref, q_ref, kn_ref, vn_ref, *refs, pp, n_blocks):
    kt_refs, vt_refs = refs[:pp], refs[pp:2 * pp]
    o_ref = refs[2 * pp]
    qf_s, m_s, l_s, g_s, o_s = refs[2 * pp + 1:]
    gi = pl.program_id(1)
    t = q_ref.shape[0]
    kw, page = kt_refs[0].shape
    per_block = MOBA_BLOCK // page
    rows = B_HEADS * t

    @pl.when(gi == 0)
    def _():
        q = q_ref[...]
        zero = jnp.zeros((t, LANES), F32)
        for kv in range(B_KV_HEADS):
            for g in range(B_GROUP):
                piece = _half_place(q[:, kv * LANES:(kv + 1) * LANES], g, kv % 2)
                full = jnp.concatenate([piece if c == kv // 2 else zero for c in range(kw // LANES)], axis=1)
                r0 = (kv * B_GROUP + g) * t
                qf_s[r0:r0 + t] = full
        m_s[...] = jnp.full_like(m_s, NEG)
        l_s[...] = jnp.zeros_like(l_s)
        g_s[...] = jnp.zeros_like(g_s)

    qs = (qf_s[...] * SCALE).astype(MXU_DTYPE)
    lane = _lane_iota((rows, LANES))
    n_here = pp // per_block
    block_pages = lambda refs, bi: jnp.concatenate(
        [refs[i][...].astype(MXU_DTYPE) for i in range(bi * per_block, (bi + 1) * per_block)], axis=1)

    scores = [_dot(qs, block_pages(kt_refs, bi)) for bi in range(n_here)]
    parts = [[s[:, i * LANES:(i + 1) * LANES] for i in range(MOBA_BLOCK // LANES)] for s in scores]
    maxes = [jnp.max(functools.reduce(jnp.maximum, pt), axis=-1, keepdims=True) for pt in parts]
    probs = [[jnp.exp(p - mb) for p in pt] for pt, mb in zip(parts, maxes)]
    sums = [jnp.sum(functools.reduce(jnp.add, ps), axis=-1, keepdims=True) for ps in probs]
    gates = [jnp.sum(functools.reduce(jnp.add, pt), axis=-1, keepdims=True) * (1.0 / (MOBA_BLOCK * SCALE))
             for pt in parts]
    m_all, l_all, g_all = m_s[...], l_s[...], g_s[...]
    for bi in range(n_here):
        jb = gi * n_here + bi
        o_s[jb] = _dot_nt(jnp.concatenate(probs[bi], axis=1).astype(MXU_DTYPE), block_pages(vt_refs, bi))
        here = lane == jb
        m_all = jnp.where(here, maxes[bi], m_all)
        l_all = jnp.where(here, sums[bi], l_all)
        g_all = jnp.where(here, gates[bi], g_all)
    m_s[...] = m_all
    l_s[...] = l_all
    g_s[...] = g_all

    @pl.when(gi == pl.num_programs(1) - 1)
    def _():
        pad = jnp.zeros((LANES - t, kw), F32)
        kn = jnp.concatenate([kn_ref[...], pad], axis=0).astype(MXU_DTYPE)
        vn = jnp.concatenate([vn_ref[...], pad], axis=0).astype(MXU_DTYPE)
        sn = _dot_nt(qs, kn)
        sn = jnp.where(_lane_iota(sn.shape) <= (_row_iota(sn.shape) & (t - 1)), sn, NEG)
        m_own = jnp.max(sn, axis=-1, keepdims=True)
        p_own = jnp.exp(sn - m_own)
        l_own = jnp.sum(p_own, axis=-1, keepdims=True)
        o_own = _dot(p_own.astype(MXU_DTYPE), vn)

        nbp = -(-n_blocks // SUBLANES) * SUBLANES
        gate_t = jnp.transpose(jnp.concatenate([g_s[...], jnp.zeros((LANES - rows, LANES), F32)], axis=0))[0:nbp]
        sel_t = _topk_select_rows(gate_t, _row_iota(gate_t.shape) < n_blocks, n_blocks).astype(F32)
        sel_t = jnp.concatenate([sel_t, jnp.zeros((LANES - nbp, LANES), F32)], axis=0)
        sel = jnp.transpose(sel_t)[0:rows] > 0.0
        mall = m_s[...]
        mx = jnp.maximum(jnp.max(jnp.where(sel, mall, NEG), axis=-1, keepdims=True), m_own)
        w = jnp.where(sel, jnp.exp(mall - mx), 0.0)
        w_own = jnp.exp(m_own - mx)
        den = jnp.sum(w * l_s[...], axis=-1, keepdims=True) + w_own * l_own
        num = w_own * o_own
        for jb in range(n_blocks):
            num = num + w[:, jb:jb + 1] * o_s[jb]
        o = num / den
        for kv in range(B_KV_HEADS):
            r0 = kv * B_GROUP * t
            src = o[r0:r0 + B_GROUP * t, (kv // 2) * LANES:(kv // 2 + 1) * LANES]
            o_ref[:, kv * LANES:(kv + 1) * LANES] = _merge_halves(src[0:t], kv % 2, src[t:2 * t], kv % 2)


def _moba_decode_call(q, kn, vn, cache_kt, cache_vt, layer, pt_flat, n_pages, pp):
    n, t, qw = q.shape
    kw, page = cache_kt.shape[2], cache_kt.shape[3]
    per_block = MOBA_BLOCK // page
    past = n_pages * page
    n_blocks = past // MOBA_BLOCK
    assert MOBA_BLOCK % page == 0 and pp % per_block == 0 and n_pages % pp == 0 and past % MOBA_BLOCK == 0
    assert n_blocks <= LANES and t <= LANES and (t & (t - 1)) == 0 and t % SUBLANES == 0 and B_GROUP == 2
    assert page == LANES and kw == B_KV_HEADS * HEAD_DIM
    rows = B_HEADS * t
    assert rows <= LANES
    per_seq = lambda w: pl.BlockSpec((None, t, w), lambda b, g, pt: (b, 0, 0))
    grid_spec = pltpu.PrefetchScalarGridSpec(
        num_scalar_prefetch=1,
        grid=(n, n_pages // pp),
        in_specs=[per_seq(qw), per_seq(kw), per_seq(kw)]
        + _paged_specs(layer, pp, n_pages, (kw, page)) + _paged_specs(layer, pp, n_pages, (kw, page)),
        out_specs=per_seq(qw),
        scratch_shapes=[pltpu.VMEM((rows, kw), F32),
                        pltpu.VMEM((rows, LANES), F32), pltpu.VMEM((rows, LANES), F32),
                        pltpu.VMEM((rows, LANES), F32), pltpu.VMEM((n_blocks, rows, kw), F32)],
    )
    return pl.pallas_call(
        functools.partial(_moba_decode_kernel, pp=pp, n_blocks=n_blocks),
        grid_spec=grid_spec,
        out_shape=jax.ShapeDtypeStruct((n, t, qw), F32),
        compiler_params=_cparams(("arbitrary", "arbitrary"), VMEM_LIMIT),
        name="moba_attn_decode",
    )(pt_flat, q, kn, vn, *([cache_kt] * pp), *([cache_vt] * pp))


def _global_splits(k_form, vb_form):
    return ((0, A_HEADS * 2 * HEAD_DIM, True, ROWS), (512, A_KV_HEADS * 2 * HEAD_DIM, True, k_form),
            (768, A_KV_HEADS * 2 * HEAD_DIM, False, ROWS), (1024, B_HEADS * HEAD_DIM, True, ROWS),
            (1536, B_KV_HEADS * HEAD_DIM, True, k_form), (1792, B_KV_HEADS * HEAD_DIM, False, vb_form))


def _local_splits(k_form):
    return ((0, C_HEADS * HEAD_DIM, True, ROWS), (1024, C_KV_HEADS * HEAD_DIM, True, k_form),
            (1152, C_KV_HEADS * HEAD_DIM, False, ROWS))


def kernel(x_prompt, x_sample, cache_a_k, cache_a_v, cache_b_k, cache_b_v, state_c_k, state_c_v, state_ffn,
           page_table, norm_mix, w_in_g, w_out_g, diff_lambda, diff_subln, w_in_l, w_out_l, sinks, norm_ffn,
           w_up, conv_w, conv_b, w_down, norm_final):
    bsz, seq, d = x_prompt.shape
    nd, t_dec, _ = x_sample.shape
    depth = norm_mix.shape[0]
    ng, pool, page = cache_a_k.shape[:3]
    n_pages = page_table.shape[1]
    past = n_pages * page
    buf = state_c_k.shape[2]
    dff2 = w_up.shape[2]

    cast = lambda w: w.astype(MXU_DTYPE)
    w_in_g, w_out_g, w_in_l, w_out_l, w_up_c, w_down_c = map(cast, (w_in_g, w_out_g, w_in_l, w_out_l, w_up, w_down))

    kw_a = A_KV_HEADS * 2 * HEAD_DIM
    kw_b = B_KV_HEADS * HEAD_DIM
    cakt = jnp.transpose(cache_a_k, (0, 1, 3, 4, 5, 2)).reshape(ng, pool, kw_a, page)
    cav = cache_a_v.reshape(ng, pool, page * A_KV_HEADS, 2 * HEAD_DIM)
    cbkt = jnp.transpose(cache_b_k, (0, 1, 3, 4, 2)).reshape(ng, pool, kw_b, page)
    cbvt = jnp.transpose(cache_b_v, (0, 1, 3, 4, 2)).reshape(ng, pool, kw_b, page)
    pt_flat = page_table.reshape(-1)

    tab_p = _rope_tables(jnp.arange(seq, dtype=jnp.int32))
    pos_s = past + jnp.repeat(jnp.arange(t_dec, dtype=jnp.int32), nd)
    tab_s = _rope_tables(pos_s)

    n_s = nd * t_dec
    xp = x_prompt.reshape(bsz * seq, d)
    xs = jnp.swapaxes(x_sample, 0, 1).reshape(n_s, d)

    tm_p = _tile(seq, 512)
    tm_s = _tile(n_s, 512)
    tf_p = dff2 // 4 if (dff2 // 4) % LANES == 0 else LANES
    tf_s = 2 * LANES if (dff2 // 2) % (2 * LANES) == 0 else LANES
    halo_p = SUBLANES
    halo_s = (CONV_W - 1) * nd
    pp = 32 if n_pages % 32 == 0 else n_pages
    dec_blk = 8 if nd % 8 == 0 else nd
    r3 = lambda a: a.reshape(bsz, seq, -1)

    def to_seq_major(a):
        return jnp.swapaxes(a.reshape(t_dec, nd, -1), 0, 1)

    def to_pos_major(a):
        return jnp.swapaxes(a, 0, 1).reshape(n_s, -1)

    outs = {k: [] for k in ("akp", "avp", "bkp", "bvp", "ckp", "cvp", "fp", "aks", "avs", "bks", "bvs", "cks", "cvs", "fs")}
    flat = lambda a: a.reshape(bsz * seq, -1)
    for l in range(depth):
        i = l // 2
        if l % 2 == 0:
            lam_init = 0.8 - 0.6 * math.exp(-0.3 * l)
            qa, kat, va, qb, kbt, vb, vbt = _proj_call(xp, norm_mix[l], w_in_g[i], tab_p, _global_splits(COLS, BOTH),
                                                       tm_p, seq)
            oa = _diff_prompt_call(r3(qa), kat, r3(va), diff_lambda[i], diff_subln[i], lam_init, MOBA_BLOCK)
            ob = _moba_prompt_call(r3(qb), kbt, r3(vb))
            op = ((flat(oa), 0), (flat(ob), 0))
            outs["akp"].append(jnp.transpose(kat.reshape(bsz, A_KV_HEADS, 2, HEAD_DIM, seq), (0, 4, 1, 2, 3)))
            outs["avp"].append(va.reshape(bsz, seq, A_KV_HEADS, 2 * HEAD_DIM))
            outs["bkp"].append(jnp.transpose(kbt.reshape(bsz, B_KV_HEADS, HEAD_DIM, seq), (0, 3, 1, 2)))
            outs["bvp"].append(jnp.transpose(vbt.reshape(bsz, B_KV_HEADS, HEAD_DIM, seq), (0, 3, 1, 2)))

            qa, ka, va, qb, kb, vb = map(to_seq_major, _proj_call(xs, norm_mix[l], w_in_g[i], tab_s,
                                                                  _global_splits(ROWS, ROWS), tm_s, n_s))
            oa = _diff_decode_call(qa, ka, va, cakt, cav, i, pt_flat, n_pages, diff_lambda[i], diff_subln[i],
                                   lam_init, pp)
            ob = _moba_decode_call(qb, kb, vb, cbkt, cbvt, i, pt_flat, n_pages, pp)
            os_ = ((to_pos_major(oa), 0), (to_pos_major(ob), 0))
            outs["aks"].append(ka.reshape(nd, t_dec, A_KV_HEADS, 2, HEAD_DIM))
            outs["avs"].append(va.reshape(nd, t_dec, A_KV_HEADS, 2 * HEAD_DIM))
            outs["bks"].append(kb.reshape(nd, t_dec, B_KV_HEADS, HEAD_DIM))
            outs["bvs"].append(vb.reshape(nd, t_dec, B_KV_HEADS, HEAD_DIM))
            w_o = w_out_g[i]
        else:
            q, kt, v = _proj_call(xp, norm_mix[l], w_in_l[i], tab_p, _local_splits(COLS), tm_p, seq)
            op = flat(_local_prompt_call(r3(q), kt, r3(v), sinks[i]))
            op = ((op, 0), (op, 1))
            kt_tail = kt[:, :, seq - buf:].reshape(bsz, C_KV_HEADS, HEAD_DIM, buf)
            outs["ckp"].append(jnp.transpose(kt_tail, (0, 3, 1, 2)))
            outs["cvp"].append(r3(v)[:, seq - buf:].reshape(bsz, buf, C_KV_HEADS, HEAD_DIM))

            q, k, v = map(to_seq_major, _proj_call(xs, norm_mix[l], w_in_l[i], tab_s, _local_splits(ROWS), tm_s, n_s))
            o, ck, cv = _local_decode_call(q, state_c_k[i].reshape(nd, buf, LANES), state_c_v[i].reshape(nd, buf, LANES),
                                           k, v, sinks[i], dec_blk)
            os_ = to_pos_major(o)
            os_ = ((os_, 0), (os_, 1))
            outs["cks"].append(ck.reshape(nd, buf, C_KV_HEADS, HEAD_DIM))
            outs["cvs"].append(cv.reshape(nd, buf, C_KV_HEADS, HEAD_DIM))
            w_o = w_out_l[i]

        final = l == depth - 1
        prev_p = jnp.zeros((bsz, halo_p, dff2), F32)
        xp, tail_p = _ffn_call(xp, op, w_o, norm_ffn[l], prev_p, w_up_c[l], conv_w[l], conv_b[l], w_down_c[l],
                               norm_final, seq_rows=seq, halo=halo_p, shift=1, final_norm=final, tm=tm_p, tf=tf_p)
        outs["fp"].append(tail_p[:, halo_p - (CONV_W - 1):])
        prev_s = jnp.swapaxes(state_ffn[l], 0, 1).reshape(1, halo_s, dff2)
        xs, tail_s = _ffn_call(xs, os_, w_o, norm_ffn[l], prev_s, w_up_c[l], conv_w[l], conv_b[l], w_down_c[l],
                               norm_final, seq_rows=n_s, halo=halo_s, shift=nd, final_norm=final, tm=tm_s, tf=tf_s)
        outs["fs"].append(jnp.swapaxes(tail_s.reshape(CONV_W - 1, nd, dff2), 0, 1))

    st = lambda key: jnp.stack(outs[key])
    y_p = xp.reshape(bsz, seq, d)
    y_s = jnp.swapaxes(xs.reshape(t_dec, nd, d), 0, 1)
    return (y_p, y_s, st("akp"), st("avp"), st("bkp"), st("bvp"), st("ckp"), st("cvp"), st("fp"),
            st("aks"), st("avs"), st("bks"), st("bvs"), st("cks"), st("cvs"), st("fs"))
```

```python
import functools
import math

import jax
import jax.numpy as jnp
from jax import lax
from jax.experimental import pallas as pl
from jax.experimental.pallas import tpu as pltpu

HEAD_DIM = 64
SCALE = HEAD_DIM ** -0.5
ROT_DIM = HEAD_DIM // 4
ROPE_THETA = 500000.0
NORM_EPS = 1e-6
SUBLN_EPS = 1e-5
A_HEADS, A_KV_HEADS = 4, 2
A_GROUP = A_HEADS // A_KV_HEADS
B_HEADS, B_KV_HEADS = 8, 4
B_GROUP = B_HEADS // B_KV_HEADS
MOBA_BLOCK = 256
MOBA_TOPK = 3
C_HEADS, C_KV_HEADS = 16, 2
C_GROUP = C_HEADS // C_KV_HEADS
WINDOW = 128
CONV_W = 3

LANES = 128
SUBLANES = 8
HALF = LANES // 2
NEG = -1e30
ROW_CHUNK = 64
MXU_DTYPE = jnp.bfloat16
VMEM_LIMIT = 52 * 1024 * 1024

F32 = jnp.float32


def _cparams(sem, vmem=None):
    return pltpu.CompilerParams(dimension_semantics=sem, vmem_limit_bytes=vmem)


def _tile(n, pref):
    t = min(n, pref)
    while n % t or t % SUBLANES:
        t -= 1
    return t


def _lane_iota(shape):
    return lax.broadcasted_iota(jnp.int32, shape, len(shape) - 1)


def _row_iota(shape):
    return lax.broadcasted_iota(jnp.int32, shape, len(shape) - 2)


def _half_place(chunk, src_half, dst_half):
    x = chunk if src_half == dst_half else pltpu.roll(chunk, HALF, axis=1)
    lane = _lane_iota(x.shape)
    keep = (lane >= HALF) if dst_half else (lane < HALF)
    return jnp.where(keep, x, 0.0)


def _merge_halves(lo, lo_src_half, hi, hi_src_half):
    a = lo if lo_src_half == 0 else pltpu.roll(lo, HALF, axis=1)
    b = hi if hi_src_half == 1 else pltpu.roll(hi, HALF, axis=1)
    return jnp.where(_lane_iota(a.shape) < HALF, a, b)


def _dot_nt(a, b):
    return lax.dot_general(a, b, (((1,), (1,)), ((), ())), preferred_element_type=F32)


def _dot(a, b):
    return jnp.dot(a, b, preferred_element_type=F32)


def _dot_split(a, b):
    ah, bh = a.astype(MXU_DTYPE), b.astype(MXU_DTYPE)
    al, bl = (a - ah.astype(F32)).astype(MXU_DTYPE), (b - bh.astype(F32)).astype(MXU_DTYPE)
    return _dot(ah, bh) + (_dot(al, bh) + _dot(ah, bl))


def _rms(x, g, eps):
    ms = jnp.mean(x * x, axis=-1, keepdims=True)
    return x * lax.rsqrt(ms + eps) * g


def _gelu(x):
    return 0.5 * x * (1.0 + lax.erf(x * math.sqrt(0.5)))


def _diff_lambda(lp, lam_init):
    a = jnp.sum(lp[0:1] * lp[1:2], axis=-1, keepdims=True)
    b = jnp.sum(lp[2:3] * lp[3:4], axis=-1, keepdims=True)
    return jnp.exp(a) - jnp.exp(b) + lam_init


def _rope_tables(pos):
    half = ROT_DIM // 2
    inv = ROPE_THETA ** (-jnp.arange(half, dtype=F32) * 2.0 / ROT_DIM)
    ang = pos.astype(F32)[:, None] * inv[None, :]
    cos, sin = jnp.cos(ang), jnp.sin(ang)
    t = pos.shape[0]
    rest = HEAD_DIM - ROT_DIM
    zh = jnp.zeros((t, half), F32)
    c = jnp.concatenate([cos, cos, jnp.ones((t, rest), F32)], axis=-1)
    sn = jnp.concatenate([-sin, zh, jnp.zeros((t, rest), F32)], axis=-1)
    sp = jnp.concatenate([zh, sin, jnp.zeros((t, rest), F32)], axis=-1)
    rep = LANES // HEAD_DIM
    return tuple(jnp.tile(a, (1, rep)) for a in (c, sn, sp)) + (cos.T, sin.T)


ROWS, COLS, BOTH = "rows", "cols", "both"


def _proj_kernel(x_ref, g_ref, w_ref, cos_ref, sn_ref, sp_ref, cost_ref, sint_ref, *out_refs, splits):
    xn = _rms(x_ref[...], g_ref[...], NORM_EPS).astype(MXU_DTYPE)
    half = ROT_DIM // 2
    outs = iter(out_refs)
    for start, width, rotary, form in splits:
        y = _dot(xn, w_ref[:, start:start + width])
        if form in (ROWS, BOTH):
            o_ref = next(outs)
            cos, sn, sp = cos_ref[...], sn_ref[...], sp_ref[...]
            for c in range(width // LANES):
                yc = y[:, c * LANES:(c + 1) * LANES]
                if rotary:
                    yc = (yc * cos + pltpu.roll(yc, LANES - half, axis=1) * sn
                          + pltpu.roll(yc, half, axis=1) * sp)
                o_ref[:, c * LANES:(c + 1) * LANES] = yc
        if form in (COLS, BOTH):
            o_ref = next(outs)
            yt = jnp.transpose(y)
            if not rotary:
                o_ref[...] = yt
                continue
            cos_t, sin_t = cost_ref[...], sint_ref[...]
            for h in range(width // HEAD_DIM):
                r = h * HEAD_DIM
                x1, x2 = yt[r:r + half], yt[r + half:r + 2 * half]
                o_ref[r:r + half] = x1 * cos_t - x2 * sin_t
                o_ref[r + half:r + 2 * half] = x2 * cos_t + x1 * sin_t
                o_ref[r + 2 * half:r + HEAD_DIM] = yt[r + 2 * half:r + HEAD_DIM]


def _proj_call(x, g, w, tables, splits, tm, seq_rows):
    n, d = x.shape
    nt = tables[0].shape[0]
    assert n % tm == 0 and nt % tm == 0 and seq_rows % tm == 0 and ROT_DIM // 2 == SUBLANES
    ntb = nt // tm
    tps = seq_rows // tm
    row = lambda i: (i, 0)
    fixed = lambda i: (0, 0)
    in_specs = [pl.BlockSpec((tm, d), row), pl.BlockSpec((1, d), fixed), pl.BlockSpec(w.shape, fixed)]
    in_specs += [pl.BlockSpec((tm, LANES), lambda i: (i % ntb, 0))] * 3
    in_specs += [pl.BlockSpec((SUBLANES, tm), lambda i: (0, i % ntb))] * 2
    out_specs, out_shape = [], []
    for _, wd, _, form in splits:
        if form in (ROWS, BOTH):
            out_specs.append(pl.BlockSpec((tm, wd), row))
            out_shape.append(jax.ShapeDtypeStruct((n, wd), F32))
        if form in (COLS, BOTH):
            out_specs.append(pl.BlockSpec((None, wd, tm), lambda i: (i // tps, 0, i % tps)))
            out_shape.append(jax.ShapeDtypeStruct((n // seq_rows, wd, seq_rows), F32))
    return pl.pallas_call(
        functools.partial(_proj_kernel, splits=splits),
        grid=(n // tm,),
        in_specs=in_specs,
        out_specs=out_specs,
        out_shape=out_shape,
        compiler_params=_cparams(("arbitrary",), VMEM_LIMIT),
        name="norm_proj_rope",
    )(x, g.reshape(1, d), w, *tables)


def _ffn_kernel(x_ref, o1_ref, o2_ref, wo1_ref, wo2_ref, g_ref, pa_ref, pb_ref, wa_ref, wb_ref, cwa_ref, cwb_ref,
                cba_ref, cbb_ref, wd_ref, gf_ref, xo_ref, sa_ref, sb_ref,
                x1_s, hn_s, acc_s, ua_s, ub_s, ha_s, hb_s, *, tiles_per_seq, halo, shift, final_norm):
    i, j = pl.program_id(0), pl.program_id(1)
    nj = pl.num_programs(1)
    tm = x_ref.shape[0]

    @pl.when(j == 0)
    def _():
        x1 = (x_ref[...] + _dot(o1_ref[...].astype(MXU_DTYPE), wo1_ref[...])
              + _dot(o2_ref[...].astype(MXU_DTYPE), wo2_ref[...]))
        x1_s[...] = x1
        hn_s[...] = _rms(x1, g_ref[...], NORM_EPS).astype(MXU_DTYPE)
        acc_s[...] = jnp.zeros_like(acc_s)

    first = (i % tiles_per_seq) == 0

    @pl.when(first)
    def _():
        ua_s[0:halo] = pa_ref[...]
        ub_s[0:halo] = pb_ref[...]

    @pl.when(jnp.logical_not(first))
    def _():
        ua_s[0:halo] = ha_s[j]
        ub_s[0:halo] = hb_s[j]

    hn = hn_s[...]
    ua_s[halo:halo + tm] = _dot(hn, wa_ref[...])
    ub_s[halo:halo + tm] = _dot(hn, wb_ref[...])

    def conv(u_s, cw_ref, cb_ref):
        c = cb_ref[...]
        for tap in range(CONV_W):
            off = halo - (CONV_W - 1 - tap) * shift
            c = c + u_s[off:off + tm] * cw_ref[tap:tap + 1]
        return c

    a = conv(ua_s, cwa_ref, cba_ref)
    b = conv(ub_s, cwb_ref, cbb_ref)
    act = (_gelu(a) * b).astype(MXU_DTYPE)
    acc_s[...] += _dot(act, wd_ref[...])

    tail_a = ua_s[tm:tm + halo]
    tail_b = ub_s[tm:tm + halo]
    ha_s[j] = tail_a
    hb_s[j] = tail_b
    sa_ref[...] = tail_a
    sb_ref[...] = tail_b

    @pl.when(j == nj - 1)
    def _():
        xo = x1_s[...] + acc_s[...]
        if final_norm:
            xo = _rms(xo, gf_ref[...], NORM_EPS)
        xo_ref[...] = xo


def _ffn_call(x, o_parts, w_out, g, prev, w_up, conv_w, conv_b, w_down, g_final, *, seq_rows, halo, shift,
              final_norm, tm, tf):
    n, d = x.shape
    dff = w_down.shape[0]
    half = w_out.shape[0] // 2
    assert seq_rows % tm == 0 and dff % tf == 0 and tm >= halo
    tiles_per_seq = seq_rows // tm
    nj = dff // tf
    row = lambda i, j: (i, 0)
    fixed = lambda i, j: (0, 0)
    (o1, c1), (o2, c2) = o_parts
    in_specs = [
        pl.BlockSpec((tm, d), row),
        pl.BlockSpec((tm, half), lambda i, j: (i, c1)), pl.BlockSpec((tm, half), lambda i, j: (i, c2)),
        pl.BlockSpec((half, d), fixed), pl.BlockSpec((half, d), lambda i, j: (1, 0)), pl.BlockSpec((1, d), fixed),
        pl.BlockSpec((None, halo, tf), lambda i, j: (i // tiles_per_seq, 0, j)),
        pl.BlockSpec((None, halo, tf), lambda i, j: (i // tiles_per_seq, 0, nj + j)),
        pl.BlockSpec((d, tf), lambda i, j: (0, j)), pl.BlockSpec((d, tf), lambda i, j: (0, nj + j)),
        pl.BlockSpec((CONV_W, tf), lambda i, j: (0, j)), pl.BlockSpec((CONV_W, tf), lambda i, j: (0, nj + j)),
        pl.BlockSpec((1, tf), lambda i, j: (0, j)), pl.BlockSpec((1, tf), lambda i, j: (0, nj + j)),
        pl.BlockSpec((tf, d), lambda i, j: (j, 0)), pl.BlockSpec((1, d), fixed),
    ]
    tail_spec = pl.BlockSpec((None, halo, tf), lambda i, j: (i, 0, j))
    xo, ta, tb = pl.pallas_call(
        functools.partial(_ffn_kernel, tiles_per_seq=tiles_per_seq, halo=halo, shift=shift,
                          final_norm=final_norm),
        grid=(n // tm, nj),
        in_specs=in_specs,
        out_specs=[pl.BlockSpec((tm, d), row), tail_spec, tail_spec],
        out_shape=[jax.ShapeDtypeStruct((n, d), F32), jax.ShapeDtypeStruct((n // tm, halo, dff), F32),
                   jax.ShapeDtypeStruct((n // tm, halo, dff), F32)],
        scratch_shapes=[
            pltpu.VMEM((tm, d), F32), pltpu.VMEM((tm, d), MXU_DTYPE), pltpu.VMEM((tm, d), F32),
            pltpu.VMEM((halo + tm, tf), F32), pltpu.VMEM((halo + tm, tf), F32),
            pltpu.VMEM((nj, halo, tf), F32), pltpu.VMEM((nj, halo, tf), F32),
        ],
        compiler_params=_cparams(("arbitrary", "arbitrary"), VMEM_LIMIT),
        name="outproj_convffn",
    )(x, o1, o2, w_out, w_out, g.reshape(1, d), prev, prev, w_up, w_up, conv_w, conv_w,
      conv_b.reshape(1, -1), conv_b.reshape(1, -1), w_down, g_final.reshape(1, d))
    last = slice(tiles_per_seq - 1, None, tiles_per_seq)
    return xo, jnp.concatenate([ta[last], tb[last]], axis=-1)


def _flash_update(qs_s, kt, v, m_s, l_s, acc_s, mask_fn):
    tk = kt.shape[1]
    for r0 in range(0, qs_s.shape[0], ROW_CHUNK):
        rs = slice(r0, r0 + ROW_CHUNK)
        s = _dot(qs_s[rs], kt)
        if mask_fn is not None:
            s = mask_fn(r0, s)
        parts = [s[:, i * LANES:(i + 1) * LANES] for i in range(tk // LANES)]
        mx = jnp.max(functools.reduce(jnp.maximum, parts), axis=-1, keepdims=True)
        m_prev = m_s[rs]
        m_new = jnp.maximum(m_prev, mx)
        alpha = jnp.exp(m_prev - m_new)
        ps = [jnp.exp(p - m_new) for p in parts]
        l_s[rs] = alpha * l_s[rs] + jnp.sum(functools.reduce(jnp.add, ps), axis=-1, keepdims=True)
        acc_s[rs] = alpha * acc_s[rs] + _dot(jnp.concatenate(ps, axis=1).astype(MXU_DTYPE), v)
        m_s[rs] = m_new


def _causal_mask(tq):
    def mask(r0, s):
        r = (_row_iota(s.shape) + r0) & (tq - 1)
        return jnp.where(_lane_iota(s.shape) <= r, s, NEG)
    return mask


def _diff_finish(acc, l, lam, g, lam_init, rows):
    o = acc[0:rows] / l[0:rows] - lam * (acc[rows:2 * rows] / l[rows:2 * rows])
    return _rms(o, g, SUBLN_EPS) * (1.0 - lam_init)


def _diff_prompt_kernel(q_ref, kt_ref, v_ref, lam_ref, g_ref, o_ref, qs_s, m_s, l_s, acc_s, *, lam_init):
    qi = pl.program_id(2)
    tq = q_ref.shape[0]
    rows = A_GROUP * tq
    q = q_ref[...]
    lane = _lane_iota((tq, LANES))
    for m in range(2):
        for g in range(A_GROUP):
            chunk = q[:, g * LANES:(g + 1) * LANES]
            keep = (lane >= HALF) if m else (lane < HALF)
            r0 = (m * A_GROUP + g) * tq
            qs_s[r0:r0 + tq] = (jnp.where(keep, chunk, 0.0) * SCALE).astype(MXU_DTYPE)
    m_s[...] = jnp.full_like(m_s, NEG)
    l_s[...] = jnp.zeros_like(l_s)
    acc_s[...] = jnp.zeros_like(acc_s)

    def step(j, mask_fn):
        start = pl.multiple_of(j * tq, tq)
        kt = kt_ref[:, pl.ds(start, tq)].astype(MXU_DTYPE)
        v = v_ref[pl.ds(start, tq), :].astype(MXU_DTYPE)
        _flash_update(qs_s, kt, v, m_s, l_s, acc_s, mask_fn)

    step(qi, _causal_mask(tq))
    lax.fori_loop(0, qi, lambda j, c: (step(j, None), c)[1], 0)

    lam = _diff_lambda(lam_ref[...], lam_init)
    o = _diff_finish(acc_s[...], l_s[...], lam, g_ref[...], lam_init, rows)
    for g in range(A_GROUP):
        o_ref[:, g * LANES:(g + 1) * LANES] = o[g * tq:(g + 1) * tq]


def _diff_prompt_call(qa, kat, va, lam_p, subln_g, lam_init, tq):
    b, s, _ = qa.shape
    assert s % tq == 0 and (tq & (tq - 1)) == 0 and (2 * A_GROUP * tq) % ROW_CHUNK == 0
    kvw = 2 * HEAD_DIM
    qw = A_GROUP * kvw
    rows = 2 * A_GROUP * tq
    return pl.pallas_call(
        functools.partial(_diff_prompt_kernel, lam_init=lam_init),
        grid=(b, A_KV_HEADS, s // tq),
        in_specs=[
            pl.BlockSpec((None, tq, qw), lambda bi, h, i: (bi, i, h)),
            pl.BlockSpec((None, kvw, s), lambda bi, h, i: (bi, h, 0)),
            pl.BlockSpec((None, s, kvw), lambda bi, h, i: (bi, 0, h)),
            pl.BlockSpec(lam_p.shape, lambda bi, h, i: (0, 0)),
            pl.BlockSpec((1, kvw), lambda bi, h, i: (0, 0)),
        ],
        out_specs=pl.BlockSpec((None, tq, qw), lambda bi, h, i: (bi, i, h)),
        out_shape=jax.ShapeDtypeStruct((b, s, A_KV_HEADS * qw), F32),
        scratch_shapes=[pltpu.VMEM((rows, LANES), MXU_DTYPE), pltpu.VMEM((rows, LANES), F32),
                        pltpu.VMEM((rows, LANES), F32), pltpu.VMEM((rows, LANES), F32)],
        compiler_params=_cparams(("arbitrary", "arbitrary", "arbitrary"), VMEM_LIMIT),
        name="diff_attn_prompt",
    )(qa, kat, va, lam_p, subln_g.reshape(1, kvw))


def _topk_select_rows(gate_t, valid, n):
    gm = jnp.where(valid, gate_t, -jnp.inf)
    idx = _row_iota(gm.shape)
    rank = jnp.zeros(gm.shape, F32)
    for i in range(n):
        gi = gm[i:i + 1, :]
        beats = (gi > gm) | ((gi == gm) & (idx > i))
        rank = rank + beats.astype(F32)
    return valid & (rank < MOBA_TOPK)


def _moba_prompt_kernel(q_ref, kt_ref, v_ref, o_ref, qs_s, qf_s, km_s, sel_s, m_s, l_s, acc_s, *, nb):
    qi = pl.program_id(2)
    tq = q_ref.shape[0]
    nbp = -(-nb // SUBLANES) * SUBLANES
    q = q_ref[...]
    for kvp in range(2):
        for g in range(B_GROUP):
            piece = _half_place(q[:, kvp * LANES:(kvp + 1) * LANES], g, kvp)
            r0 = (kvp * B_GROUP + g) * tq
            qf_s[r0:r0 + tq] = piece
            qs_s[r0:r0 + tq] = (piece * SCALE).astype(MXU_DTYPE)

    @pl.when(qi == 0)
    def _():
        col_block = _lane_iota((LANES, LANES)) & (nb - 1)
        km = jnp.zeros((LANES, LANES), F32)
        for n in range(nb):
            mean_n = jnp.sum(kt_ref[:, n * MOBA_BLOCK:(n + 1) * MOBA_BLOCK], axis=-1, keepdims=True) * (1.0 / MOBA_BLOCK)
            km = jnp.where(col_block == n, mean_n, km)
        km_s[...] = km

    m_s[...] = jnp.full_like(m_s, NEG)
    l_s[...] = jnp.zeros_like(l_s)
    acc_s[...] = jnp.zeros_like(acc_s)

    gate = _dot_split(qf_s[...], km_s[...])
    gate_t = jnp.transpose(gate)[0:nbp]
    sel_t = _topk_select_rows(gate_t, _row_iota(gate_t.shape) < jnp.minimum(qi, nb), nb).astype(F32)
    weights = jnp.left_shift(1, _row_iota(sel_t.shape)).astype(F32)
    bits_row = jnp.sum(sel_t * weights, axis=0, keepdims=True)
    bits = jnp.transpose(jnp.broadcast_to(bits_row, (LANES, gate.shape[0])))
    sel_s[...] = bits.astype(jnp.int32)

    def block_step(back, carry):
        j = qi - back
        start = pl.multiple_of(j * tq, tq)
        kt = kt_ref[:, pl.ds(start, tq)].astype(MXU_DTYPE)
        v = v_ref[pl.ds(start, tq), :].astype(MXU_DTYPE)

        def mask(r0, s):
            picked = jnp.right_shift(sel_s[r0:r0 + ROW_CHUNK], j) & 1
            lane = _lane_iota(picked.shape)
            own_limit = (_row_iota(picked.shape) + r0) & (tq - 1)
            limit = jnp.where(back == 0, own_limit, jnp.where(picked != 0, tq, -1))
            return jnp.concatenate([jnp.where(lane + i * LANES <= limit, s[:, i * LANES:(i + 1) * LANES], NEG)
                                    for i in range(s.shape[1] // LANES)], axis=1)

        _flash_update(qs_s, kt, v, m_s, l_s, acc_s, mask)
        return carry

    lax.fori_loop(0, qi + 1, block_step, 0)

    o = acc_s[...] / l_s[...]
    for kvp in range(2):
        r0 = kvp * B_GROUP * tq
        o_ref[:, kvp * LANES:(kvp + 1) * LANES] = _merge_halves(o[r0:r0 + tq], kvp, o[r0 + tq:r0 + 2 * tq], kvp)


def _moba_prompt_call(qb, kbt, vb):
    b, s, _ = qb.shape
    tq = MOBA_BLOCK
    nb = s // tq
    assert s % tq == 0 and LANES % nb == 0 and nb <= 24 and B_GROUP == 2
    pairs = B_KV_HEADS // 2
    qw = 2 * B_GROUP * HEAD_DIM
    rows = 2 * B_GROUP * tq
    assert rows % ROW_CHUNK == 0
    return pl.pallas_call(
        functools.partial(_moba_prompt_kernel, nb=nb),
        grid=(b, pairs, nb),
        in_specs=[
            pl.BlockSpec((None, tq, qw), lambda bi, h, i: (bi, i, h)),
            pl.BlockSpec((None, LANES, s), lambda bi, h, i: (bi, h, 0)),
            pl.BlockSpec((None, s, LANES), lambda bi, h, i: (bi, 0, h)),
        ],
        out_specs=pl.BlockSpec((None, tq, qw), lambda bi, h, i: (bi, i, h)),
        out_shape=jax.ShapeDtypeStruct((b, s, pairs * qw), F32),
        scratch_shapes=[pltpu.VMEM((rows, LANES), MXU_DTYPE), pltpu.VMEM((rows, LANES), F32),
                        pltpu.VMEM((LANES, LANES), F32), pltpu.VMEM((rows, LANES), jnp.int32),
                        pltpu.VMEM((rows, LANES), F32), pltpu.VMEM((rows, LANES), F32),
                        pltpu.VMEM((rows, LANES), F32)],
        compiler_params=_cparams(("arbitrary", "arbitrary", "arbitrary"), VMEM_LIMIT),
        name="moba_attn_prompt",
    )(qb, kbt, vb)


def _window_q(q, rows):
    pieces = []
    for h in range(C_HEADS):
        chunk = q[:, (h // 2) * LANES:(h // 2 + 1) * LANES]
        pieces.append(_half_place(chunk, h % 2, h // C_GROUP))
    return (jnp.concatenate(pieces, axis=0) * SCALE).astype(MXU_DTYPE)


def _sink_column(sink_ref, rows):
    return jnp.concatenate([jnp.full((rows, 1), sink_ref[h], F32) for h in range(C_HEADS)], axis=0)


def _window_out(o, rows):
    chunks = []
    for c in range(C_HEADS // 2):
        kv = (2 * c) // C_GROUP
        lo = o[(2 * c) * rows:(2 * c + 1) * rows]
        hi = o[(2 * c + 1) * rows:(2 * c + 2) * rows]
        chunks.append(_merge_halves(lo, kv, hi, kv))
    return chunks


def _sink_softmax_pv(s, sink, v):
    m = jnp.maximum(jnp.max(s, axis=-1, keepdims=True), sink)
    p = jnp.exp(s - m)
    den = jnp.sum(p, axis=-1, keepdims=True) + jnp.exp(sink - m)
    return _dot(p.astype(MXU_DTYPE), v) / den


def _local_prompt_kernel(sink_ref, q_ref, ktp_ref, ktc_ref, vp_ref, vc_ref, o_ref):
    has_prev = pl.program_id(1) > 0
    kt = jnp.concatenate([ktp_ref[...], ktc_ref[...]], axis=1).astype(MXU_DTYPE)
    v = jnp.concatenate([vp_ref[...], vc_ref[...]], axis=0).astype(MXU_DTYPE)
    col = _lane_iota((WINDOW, 2 * WINDOW))
    rel = WINDOW + _row_iota(col.shape) - col
    visible = (rel >= 0) & (rel < WINDOW) & ((col >= WINDOW) | has_prev)
    for c in range(C_HEADS // 2):
        kv = (2 * c) // C_GROUP
        chunk = q_ref[:, c * LANES:(c + 1) * LANES]
        outs = []
        for hh in range(2):
            sink = sink_ref[2 * c + hh]
            qh = (_half_place(chunk, hh, kv) * SCALE).astype(MXU_DTYPE)
            s = jnp.where(visible, _dot(qh, kt), NEG)
            parts = [s[:, i * LANES:(i + 1) * LANES] for i in range(2 * WINDOW // LANES)]
            m = jnp.maximum(jnp.max(functools.reduce(jnp.maximum, parts), axis=-1, keepdims=True), sink)
            ps = [jnp.exp(p - m) for p in parts]
            den = jnp.sum(functools.reduce(jnp.add, ps), axis=-1, keepdims=True) + jnp.exp(sink - m)
            outs.append(_dot(jnp.concatenate(ps, axis=1).astype(MXU_DTYPE), v) / den)
        o_ref[:, c * LANES:(c + 1) * LANES] = _merge_halves(outs[0], kv, outs[1], kv)


def _local_prompt_call(q, kt, v, sinks):
    b, s, qw = q.shape
    assert s % WINDOW == 0 and C_KV_HEADS * HEAD_DIM == LANES and WINDOW == LANES
    prev = lambda i: jnp.maximum(i - 1, 0)
    return pl.pallas_call(
        _local_prompt_kernel,
        grid=(b, s // WINDOW),
        in_specs=[
            pl.BlockSpec(memory_space=pltpu.SMEM),
            pl.BlockSpec((None, WINDOW, qw), lambda bi, i: (bi, i, 0)),
            pl.BlockSpec((None, LANES, WINDOW), lambda bi, i: (bi, 0, prev(i))),
            pl.BlockSpec((None, LANES, WINDOW), lambda bi, i: (bi, 0, i)),
            pl.BlockSpec((None, WINDOW, LANES), lambda bi, i: (bi, prev(i), 0)),
            pl.BlockSpec((None, WINDOW, LANES), lambda bi, i: (bi, i, 0)),
        ],
        out_specs=pl.BlockSpec((None, WINDOW, qw), lambda bi, i: (bi, i, 0)),
        out_shape=jax.ShapeDtypeStruct((b, s, qw), F32),
        compiler_params=_cparams(("arbitrary", "arbitrary"), VMEM_LIMIT),
        name="window_attn_prompt",
    )(sinks, q, kt, kt, v, v)


def _local_decode_kernel(sink_ref, q_ref, ck_ref, cv_ref, kn_ref, vn_ref, o_ref, ko_ref, vo_ref):
    nseq, t, _ = q_ref.shape
    buf = ck_ref.shape[1]
    sink = _sink_column(sink_ref, t)
    pad = jnp.zeros((2 * WINDOW - buf - t, LANES), F32)
    for sq in range(nseq):
        kn, vn = kn_ref[sq], vn_ref[sq]
        k = jnp.concatenate([ck_ref[sq], kn, pad], axis=0).astype(MXU_DTYPE)
        v = jnp.concatenate([cv_ref[sq], vn, pad], axis=0).astype(MXU_DTYPE)
        s = _dot_nt(_window_q(q_ref[sq], t), k)
        col = _lane_iota(s.shape)
        rel = buf + (_row_iota(s.shape) & (t - 1)) - col
        s = jnp.where((rel >= 0) & (rel < WINDOW) & (col < buf + t), s, NEG)
        o = _sink_softmax_pv(s, sink, v)
        for c, chunk in enumerate(_window_out(o, t)):
            o_ref[sq, :, c * LANES:(c + 1) * LANES] = chunk
        ko_ref[sq, 0:buf - t] = ck_ref[sq, t:buf]
        ko_ref[sq, buf - t:buf] = kn
        vo_ref[sq, 0:buf - t] = cv_ref[sq, t:buf]
        vo_ref[sq, buf - t:buf] = vn


def _local_decode_call(q, ck, cv, kn, vn, sinks, nseq):
    n, t, qw = q.shape
    buf = ck.shape[1]
    assert n % nseq == 0 and buf + t <= 2 * WINDOW and (t & (t - 1)) == 0 and t % SUBLANES == 0
    blk = lambda shape: pl.BlockSpec((nseq,) + shape, lambda i: (i, 0, 0))
    return pl.pallas_call(
        _local_decode_kernel,
        grid=(n // nseq,),
        in_specs=[pl.BlockSpec(memory_space=pltpu.SMEM), blk((t, qw)), blk((buf, LANES)), blk((buf, LANES)),
                  blk((t, LANES)), blk((t, LANES))],
        out_specs=[blk((t, qw)), blk((buf, LANES)), blk((buf, LANES))],
        out_shape=[jax.ShapeDtypeStruct((n, t, qw), F32), jax.ShapeDtypeStruct((n, buf, LANES), F32),
                   jax.ShapeDtypeStruct((n, buf, LANES), F32)],
        compiler_params=_cparams(("arbitrary",), VMEM_LIMIT),
        name="window_attn_decode",
    )(sinks, q, ck, cv, kn, vn)


def _diff_decode_kernel(pt_ref, q_ref, kn_ref, vn_ref, lam_ref, g_ref, *refs, pp, lam_init):
    kt_refs, v_refs = refs[:pp], refs[pp:2 * pp]
    o_ref = refs[2 * pp]
    qs_s, m_s, l_s, acc_s = refs[2 * pp + 1:]
    gi = pl.program_id(1)
    t = q_ref.shape[0]
    kw, page = kt_refs[0].shape
    gr = 2 * A_GROUP * t

    @pl.when(gi == 0)
    def _():
        q = q_ref[...]
        lane = _lane_iota((t, LANES))
        zero = jnp.zeros((t, LANES), F32)
        for kv in range(A_KV_HEADS):
            for m in range(2):
                for g in range(A_GROUP):
                    c = kv * A_GROUP + g
                    keep = (lane >= HALF) if m else (lane < HALF)
                    piece = jnp.where(keep, q[:, c * LANES:(c + 1) * LANES], 0.0) * SCALE
                    full = jnp.concatenate([piece if kk == kv else zero for kk in range(A_KV_HEADS)], axis=1)
                    r0 = ((kv * 2 + m) * A_GROUP + g) * t
                    qs_s[r0:r0 + t] = full
        m_s[...] = jnp.full_like(m_s, NEG)
        l_s[...] = jnp.zeros_like(l_s)
        acc_s[...] = jnp.zeros_like(acc_s)

    def update(s, values_of_head):
        parts = [s[:, i * LANES:(i + 1) * LANES] for i in range(s.shape[1] // LANES)]
        mx = jnp.max(functools.reduce(jnp.maximum, parts), axis=-1, keepdims=True)
        m_prev = m_s[...]
        m_new = jnp.maximum(m_prev, mx)
        alpha = jnp.exp(m_prev - m_new)
        ps = [jnp.exp(p - m_new) for p in parts]
        l_s[...] = alpha * l_s[...] + jnp.sum(functools.reduce(jnp.add, ps), axis=-1, keepdims=True)
        p = jnp.concatenate(ps, axis=1).astype(MXU_DTYPE)
        pv = jnp.concatenate([_dot(p[kv * gr:(kv + 1) * gr], values_of_head(kv)) for kv in range(A_KV_HEADS)], axis=0)
        acc_s[...] = alpha * acc_s[...] + pv
        m_s[...] = m_new

    def paged_values(kv):
        return jnp.concatenate([vr[pl.ds(kv, page, stride=A_KV_HEADS), :].astype(MXU_DTYPE) for vr in v_refs], axis=0)

    qs = qs_s[...].astype(MXU_DTYPE)
    kt = jnp.concatenate([kr[...].astype(MXU_DTYPE) for kr in kt_refs], axis=1)
    update(_dot(qs, kt), paged_values)

    @pl.when(gi == pl.num_programs(1) - 1)
    def _():
        pad = jnp.zeros((LANES - t, kw), F32)
        kn = jnp.concatenate([kn_ref[...], pad], axis=0).astype(MXU_DTYPE)
        vn = jnp.concatenate([vn_ref[...], pad], axis=0).astype(MXU_DTYPE)
        sn = _dot_nt(qs, kn)
        sn = jnp.where(_lane_iota(sn.shape) <= (_row_iota(sn.shape) & (t - 1)), sn, NEG)
        update(sn, lambda kv: vn[:, kv * LANES:(kv + 1) * LANES])
        lam = _diff_lambda(lam_ref[...], lam_init)
        acc, l = acc_s[...], l_s[...]
        for kv in range(A_KV_HEADS):
            o = _diff_finish(acc[kv * gr:(kv + 1) * gr], l[kv * gr:(kv + 1) * gr], lam, g_ref[...], lam_init,
                             A_GROUP * t)
            for g in range(A_GROUP):
                c = kv * A_GROUP + g
                o_ref[:, c * LANES:(c + 1) * LANES] = o[g * t:(g + 1) * t]


def _paged_specs(layer, pp, n_pages, shape):
    def make(i):
        return pl.BlockSpec((None, None) + shape, lambda b, g, pt: (layer, pt[b * n_pages + g * pp + i], 0, 0))
    return [make(i) for i in range(pp)]


def _diff_decode_call(q, kn, vn, cache_kt, cache_v, layer, pt_flat, n_pages, lam_p, subln_g, lam_init, pp):
    n, t, qw = q.shape
    kw, page = cache_kt.shape[2], cache_kt.shape[3]
    assert n_pages % pp == 0 and (t & (t - 1)) == 0 and t % SUBLANES == 0 and t <= LANES and page == LANES
    assert cache_v.shape[2:] == (page * A_KV_HEADS, 2 * HEAD_DIM)
    rows = A_KV_HEADS * 2 * A_GROUP * t
    per_seq = lambda w: pl.BlockSpec((None, t, w), lambda b, g, pt: (b, 0, 0))
    fixed = lambda shape: pl.BlockSpec(shape, lambda b, g, pt: (0, 0))
    grid_spec = pltpu.PrefetchScalarGridSpec(
        num_scalar_prefetch=1,
        grid=(n, n_pages // pp),
        in_specs=[per_seq(qw), per_seq(kw), per_seq(kw), fixed(lam_p.shape), fixed((1, LANES))]
        + _paged_specs(layer, pp, n_pages, (kw, page))
        + _paged_specs(layer, pp, n_pages, (page * A_KV_HEADS, 2 * HEAD_DIM)),
        out_specs=per_seq(qw),
        scratch_shapes=[pltpu.VMEM((rows, kw), F32), pltpu.VMEM((rows, LANES), F32),
                        pltpu.VMEM((rows, LANES), F32), pltpu.VMEM((rows, LANES), F32)],
    )
    return pl.pallas_call(
        functools.partial(_diff_decode_kernel, pp=pp, lam_init=lam_init),
        grid_spec=grid_spec,
        out_shape=jax.ShapeDtypeStruct((n, t, qw), F32),
        compiler_params=_cparams(("arbitrary", "arbitrary"), VMEM_LIMIT),
        name="diff_attn_decode",
    )(pt_flat, q, kn, vn, lam_p, subln_g.reshape(1, LANES), *([cache_kt] * pp), *([cache_v] * pp))


def _moba_decode_kernel(pt_ref, q_ref, kn_ref, vn_ref, *refs, pp, n_blocks):
    kt_refs, vt_refs = refs[:pp], refs[pp:2 * pp]
    o_ref = refs[2 * pp]
    qf_s, m_s, l_s, g_s, o_s = refs[2 * pp + 1:]
    gi = pl.program_id(1)
    t = q_ref.shape[0]
    kw, page = kt_refs[0].shape
    per_block = MOBA_BLOCK // page
    rows = B_HEADS * t

    @pl.when(gi == 0)
    def _():
        q = q_ref[...]
        zero = jnp.zeros((t, LANES), F32)
        for kv in range(B_KV_HEADS):
            for g in range(B_GROUP):
                piece = _half_place(q[:, kv * LANES:(kv + 1) * LANES], g, kv % 2)
                full = jnp.concatenate([piece if c == kv // 2 else zero for c in range(kw // LANES)], axis=1)
                r0 = (kv * B_GROUP + g) * t
                qf_s[r0:r0 + t] = full
        m_s[...] = jnp.full_like(m_s, NEG)
        l_s[...] = jnp.zeros_like(l_s)
        g_s[...] = jnp.zeros_like(g_s)

    qs = (qf_s[...] * SCALE).astype(MXU_DTYPE)
    lane = _lane_iota((rows, LANES))
    n_here = pp // per_block
    block_pages = lambda refs, bi: jnp.concatenate(
        [refs[i][...].astype(MXU_DTYPE) for i in range(bi * per_block, (bi + 1) * per_block)], axis=1)

    scores = [_dot(qs, block_pages(kt_refs, bi)) for bi in range(n_here)]
    parts = [[s[:, i * LANES:(i + 1) * LANES] for i in range(MOBA_BLOCK // LANES)] for s in scores]
    maxes = [jnp.max(functools.reduce(jnp.maximum, pt), axis=-1, keepdims=True) for pt in parts]
    probs = [[jnp.exp(p - mb) for p in pt] for pt, mb in zip(parts, maxes)]
    sums = [jnp.sum(functools.reduce(jnp.add, ps), axis=-1, keepdims=True) for ps in probs]
    gates = [jnp.sum(functools.reduce(jnp.add, pt), axis=-1, keepdims=True) * (1.0 / (MOBA_BLOCK * SCALE))
             for pt in parts]
    m_all, l_all, g_all = m_s[...], l_s[...], g_s[...]
    for bi in range(n_here):
        jb = gi * n_here + bi
        o_s[jb] = _dot_nt(jnp.concatenate(probs[bi], axis=1).astype(MXU_DTYPE), block_pages(vt_refs, bi))
        here = lane == jb
        m_all = jnp.where(here, maxes[bi], m_all)
        l_all = jnp.where(here, sums[bi], l_all)
        g_all = jnp.where(here, gates[bi], g_all)
    m_s[...] = m_all
    l_s[...] = l_all
    g_s[...] = g_all

    @pl.when(gi == pl.num_programs(1) - 1)
    def _():
        pad = jnp.zeros((LANES - t, kw), F32)
        kn = jnp.concatenate([kn_ref[...], pad], axis=0).astype(MXU_DTYPE)
        vn = jnp.concatenate([vn_ref[...], pad], axis=0).astype(MXU_DTYPE)
        sn = _dot_nt(qs, kn)
        sn = jnp.where(_lane_iota(sn.shape) <= (_row_iota(sn.shape) & (t - 1)), sn, NEG)
        m_own = jnp.max(sn, axis=-1, keepdims=True)
        p_own = jnp.exp(sn - m_own)
        l_own = jnp.sum(p_own, axis=-1, keepdims=True)
        o_own = _dot(p_own.astype(MXU_DTYPE), vn)

        nbp = -(-n_blocks // SUBLANES) * SUBLANES
        gate_t = jnp.transpose(jnp.concatenate([g_s[...], jnp.zeros((LANES - rows, LANES), F32)], axis=0))[0:nbp]
        sel_t = _topk_select_rows(gate_t, _row_iota(gate_t.shape) < n_blocks, n_blocks).astype(F32)
        sel_t = jnp.concatenate([sel_t, jnp.zeros((LANES - nbp, LANES), F32)], axis=0)
        sel = jnp.transpose(sel_t)[0:rows] > 0.0
        mall = m_s[...]
        mx = jnp.maximum(jnp.max(jnp.where(sel, mall, NEG), axis=-1, keepdims=True), m_own)
        w = jnp.where(sel, jnp.exp(mall - mx), 0.0)
        w_own = jnp.exp(m_own - mx)
        den = jnp.sum(w * l_s[...], axis=-1, keepdims=True) + w_own * l_own
        num = w_own * o_own
        for jb in range(n_blocks):
            num = num + w[:, jb:jb + 1] * o_s[jb]
        o = num / den
        for kv in range(B_KV_HEADS):
            r0 = kv * B_GROUP * t
            src = o[r0:r0 + B_GROUP * t, (kv // 2) * LANES:(kv // 2 + 1) * LANES]
            o_ref[:, kv * LANES:(kv + 1) * LANES] = _merge_halves(src[0:t], kv % 2, src[t:2 * t], kv % 2)


def _moba_decode_call(q, kn, vn, cache_kt, cache_vt, layer, pt_flat, n_pages, pp):
    n, t, qw = q.shape
    kw, page = cache_kt.shape[2], cache_kt.shape[3]
    per_block = MOBA_BLOCK // page
    past = n_pages * page
    n_blocks = past // MOBA_BLOCK
    assert MOBA_BLOCK % page == 0 and pp % per_block == 0 and n_pages % pp == 0 and past % MOBA_BLOCK == 0
    assert n_blocks <= LANES and t <= LANES and (t & (t - 1)) == 0 and t % SUBLANES == 0 and B_GROUP == 2
    assert page == LANES and kw == B_KV_HEADS * HEAD_DIM
    rows = B_HEADS * t
    assert rows <= LANES
    per_seq = lambda w: pl.BlockSpec((None, t, w), lambda b, g, pt: (b, 0, 0))
    grid_spec = pltpu.PrefetchScalarGridSpec(
        num_scalar_prefetch=1,
        grid=(n, n_pages // pp),
        in_specs=[per_seq(qw), per_seq(kw), per_seq(kw)]
        + _paged_specs(layer, pp, n_pages, (kw, page)) + _paged_specs(layer, pp, n_pages, (kw, page)),
        out_specs=per_seq(qw),
        scratch_shapes=[pltpu.VMEM((rows, kw), F32),
                        pltpu.VMEM((rows, LANES), F32), pltpu.VMEM((rows, LANES), F32),
                        pltpu.VMEM((rows, LANES), F32), pltpu.VMEM((n_blocks, rows, kw), F32)],
    )
    return pl.pallas_call(
        functools.partial(_moba_decode_kernel, pp=pp, n_blocks=n_blocks),
        grid_spec=grid_spec,
        out_shape=jax.ShapeDtypeStruct((n, t, qw), F32),
        compiler_params=_cparams(("arbitrary", "arbitrary"), VMEM_LIMIT),
        name="moba_attn_decode",
    )(pt_flat, q, kn, vn, *([cache_kt] * pp), *([cache_vt] * pp))


def _global_splits(k_form, vb_form):
    return ((0, A_HEADS * 2 * HEAD_DIM, True, ROWS), (512, A_KV_HEADS * 2 * HEAD_DIM, True, k_form),
            (768, A_KV_HEADS * 2 * HEAD_DIM, False, ROWS), (1024, B_HEADS * HEAD_DIM, True, ROWS),
            (1536, B_KV_HEADS * HEAD_DIM, True, k_form), (1792, B_KV_HEADS * HEAD_DIM, False, vb_form))


def _local_splits(k_form):
    return ((0, C_HEADS * HEAD_DIM, True, ROWS), (1024, C_KV_HEADS * HEAD_DIM, True, k_form),
            (1152, C_KV_HEADS * HEAD_DIM, False, ROWS))


def kernel(x_prompt, x_sample, cache_a_k, cache_a_v, cache_b_k, cache_b_v, state_c_k, state_c_v, state_ffn,
           page_table, norm_mix, w_in_g, w_out_g, diff_lambda, diff_subln, w_in_l, w_out_l, sinks, norm_ffn,
           w_up, conv_w, conv_b, w_down, norm_final):
    bsz, seq, d = x_prompt.shape
    nd, t_dec, _ = x_sample.shape
    depth = norm_mix.shape[0]
    ng, pool, page = cache_a_k.shape[:3]
    n_pages = page_table.shape[1]
    past = n_pages * page
    buf = state_c_k.shape[2]
    dff2 = w_up.shape[2]

    cast = lambda w: w.astype(MXU_DTYPE)
    w_in_g, w_out_g, w_in_l, w_out_l, w_up_c, w_down_c = map(cast, (w_in_g, w_out_g, w_in_l, w_out_l, w_up, w_down))

    kw_a = A_KV_HEADS * 2 * HEAD_DIM
    kw_b = B_KV_HEADS * HEAD_DIM
    cakt = jnp.transpose(cache_a_k, (0, 1, 3, 4, 5, 2)).reshape(ng, pool, kw_a, page)
    cav = cache_a_v.reshape(ng, pool, page * A_KV_HEADS, 2 * HEAD_DIM)
    cbkt = jnp.transpose(cache_b_k, (0, 1, 3, 4, 2)).reshape(ng, pool, kw_b, page)
    cbvt = jnp.transpose(cache_b_v, (0, 1, 3, 4, 2)).reshape(ng, pool, kw_b, page)
    pt_flat = page_table.reshape(-1)

    tab_p = _rope_tables(jnp.arange(seq, dtype=jnp.int32))
    pos_s = past + jnp.repeat(jnp.arange(t_dec, dtype=jnp.int32), nd)
    tab_s = _rope_tables(pos_s)

    n_s = nd * t_dec
    xp = x_prompt.reshape(bsz * seq, d)
    xs = jnp.swapaxes(x_sample, 0, 1).reshape(n_s, d)

    tm_p = _tile(seq, 512)
    tm_s = _tile(n_s, 512)
    tf_p = dff2 // 4 if (dff2 // 4) % LANES == 0 else LANES
    tf_s = 2 * LANES if (dff2 // 2) % (2 * LANES) == 0 else LANES
    halo_p = SUBLANES
    halo_s = (CONV_W - 1) * nd
    pp = 64 if n_pages % 64 == 0 else n_pages
    dec_blk = 8 if nd % 8 == 0 else nd
    r3 = lambda a: a.reshape(bsz, seq, -1)

    def to_seq_major(a):
        return jnp.swapaxes(a.reshape(t_dec, nd, -1), 0, 1)

    def to_pos_major(a):
        return jnp.swapaxes(a, 0, 1).reshape(n_s, -1)

    outs = {k: [] for k in ("akp", "avp", "bkp", "bvp", "ckp", "cvp", "fp", "aks", "avs", "bks", "bvs", "cks", "cvs", "fs")}
    flat = lambda a: a.reshape(bsz * seq, -1)
    for l in range(depth):
        i = l // 2
        if l % 2 == 0:
            lam_init = 0.8 - 0.6 * math.exp(-0.3 * l)
            qa, kat, va, qb, kbt, vb, vbt = _proj_call(xp, norm_mix[l], w_in_g[i], tab_p, _global_splits(COLS, BOTH),
                                                       tm_p, seq)
            oa = _diff_prompt_call(r3(qa), kat, r3(va), diff_lambda[i], diff_subln[i], lam_init, MOBA_BLOCK)
            ob = _moba_prompt_call(r3(qb), kbt, r3(vb))
            op = ((flat(oa), 0), (flat(ob), 0))
            outs["akp"].append(jnp.transpose(kat.reshape(bsz, A_KV_HEADS, 2, HEAD_DIM, seq), (0, 4, 1, 2, 3)))
            outs["avp"].append(va.reshape(bsz, seq, A_KV_HEADS, 2 * HEAD_DIM))
            outs["bkp"].append(jnp.transpose(kbt.reshape(bsz, B_KV_HEADS, HEAD_DIM, seq), (0, 3, 1, 2)))
            outs["bvp"].append(jnp.transpose(vbt.reshape(bsz, B_KV_HEADS, HEAD_DIM, seq), (0, 3, 1, 2)))

            qa, ka, va, qb, kb, vb = map(to_seq_major, _proj_call(xs, norm_mix[l], w_in_g[i], tab_s,
                                                                  _global_splits(ROWS, ROWS), tm_s, n_s))
            oa = _diff_decode_call(qa, ka, va, cakt, cav, i, pt_flat, n_pages, diff_lambda[i], diff_subln[i],
                                   lam_init, pp)
            ob = _moba_decode_call(qb, kb, vb, cbkt, cbvt, i, pt_flat, n_pages, pp)
            os_ = ((to_pos_major(oa), 0), (to_pos_major(ob), 0))
            outs["aks"].append(ka.reshape(nd, t_dec, A_KV_HEADS, 2, HEAD_DIM))
            outs["avs"].append(va.reshape(nd, t_dec, A_KV_HEADS, 2 * HEAD_DIM))
            outs["bks"].append(kb.reshape(nd, t_dec, B_KV_HEADS, HEAD_DIM))
            outs["bvs"].append(vb.reshape(nd, t_dec, B_KV_HEADS, HEAD_DIM))
            w_o = w_out_g[i]
        else:
            q, kt, v = _proj_call(xp, norm_mix[l], w_in_l[i], tab_p, _local_splits(COLS), tm_p, seq)
            op = flat(_local_prompt_call(r3(q), kt, r3(v), sinks[i]))
            op = ((op, 0), (op, 1))
            kt_tail = kt[:, :, seq - buf:].reshape(bsz, C_KV_HEADS, HEAD_DIM, buf)
            outs["ckp"].append(jnp.transpose(kt_tail, (0, 3, 1, 2)))
            outs["cvp"].append(r3(v)[:, seq - buf:].reshape(bsz, buf, C_KV_HEADS, HEAD_DIM))

            q, k, v = map(to_seq_major, _proj_call(xs, norm_mix[l], w_in_l[i], tab_s, _local_splits(ROWS), tm_s, n_s))
            o, ck, cv = _local_decode_call(q, state_c_k[i].reshape(nd, buf, LANES), state_c_v[i].reshape(nd, buf, LANES),
                                           k, v, sinks[i], dec_blk)
            os_ = to_pos_major(o)
            os_ = ((os_, 0), (os_, 1))
            outs["cks"].append(ck.reshape(nd, buf, C_KV_HEADS, HEAD_DIM))
            outs["cvs"].append(cv.reshape(nd, buf, C_KV_HEADS, HEAD_DIM))
            w_o = w_out_l[i]

        final = l == depth - 1
        prev_p = jnp.zeros((bsz, halo_p, dff2), F32)
        xp, tail_p = _ffn_call(xp, op, w_o, norm_ffn[l], prev_p, w_up_c[l], conv_w[l], conv_b[l], w_down_c[l],
                               norm_final, seq_rows=seq, halo=halo_p, shift=1, final_norm=final, tm=tm_p, tf=tf_p)
        outs["fp"].append(tail_p[:, halo_p - (CONV_W - 1):])
        prev_s = jnp.swapaxes(state_ffn[l], 0, 1).reshape(1, halo_s, dff2)
        xs, tail_s = _ffn_call(xs, os_, w_o, norm_ffn[l], prev_s, w_up_c[l], conv_w[l], conv_b[l], w_down_c[l],
                               norm_final, seq_rows=n_s, halo=halo_s, shift=nd, final_norm=final, tm=tm_s, tf=tf_s)
        outs["fs"].append(jnp.swapaxes(tail_s.reshape(CONV_W - 1, nd, dff2), 0, 1))

    st = lambda key: jnp.stack(outs[key])
    y_p = xp.reshape(bsz, seq, d)
    y_s = jnp.swapaxes(xs.reshape(t_dec, nd, d), 0, 1)
    return (y_p, y_s, st("akp"), st("avp"), st("bkp"), st("bvp"), st("ckp"), st("cvp"), st("fp"),
            st("aks"), st("avs"), st("bks"), st("bvs"), st("cks"), st("cvs"), st("fs"))
```

```python
import functools
import math

import jax
import jax.numpy as jnp
from jax import lax
from jax.experimental import pallas as pl
from jax.experimental.pallas import tpu as pltpu

HEAD_DIM = 64
SCALE = HEAD_DIM ** -0.5
LOG2E = math.log2(math.e)
QSCALE = SCALE * LOG2E
ROT_DIM = HEAD_DIM // 4
ROPE_THETA = 500000.0
NORM_EPS = 1e-6
SUBLN_EPS = 1e-5
A_HEADS, A_KV_HEADS = 4, 2
A_GROUP = A_HEADS // A_KV_HEADS
B_HEADS, B_KV_HEADS = 8, 4
B_GROUP = B_HEADS // B_KV_HEADS
MOBA_BLOCK = 256
MOBA_TOPK = 3
C_HEADS, C_KV_HEADS = 16, 2
C_GROUP = C_HEADS // C_KV_HEADS
WINDOW = 128
CONV_W = 3

LANES = 128
SUBLANES = 8
HALF = LANES // 2
NEG = -1e30
ROW_CHUNK = 64
MXU_DTYPE = jnp.bfloat16
VMEM_LIMIT = 52 * 1024 * 1024

F32 = jnp.float32


def _cparams(sem, vmem=None):
    return pltpu.CompilerParams(dimension_semantics=sem, vmem_limit_bytes=vmem)


def _tile(n, pref):
    t = min(n, pref)
    while n % t or t % SUBLANES:
        t -= 1
    return t


def _lane_iota(shape):
    return lax.broadcasted_iota(jnp.int32, shape, len(shape) - 1)


def _row_iota(shape):
    return lax.broadcasted_iota(jnp.int32, shape, len(shape) - 2)


def _half_place(chunk, src_half, dst_half):
    x = chunk if src_half == dst_half else pltpu.roll(chunk, HALF, axis=1)
    lane = _lane_iota(x.shape)
    keep = (lane >= HALF) if dst_half else (lane < HALF)
    return jnp.where(keep, x, 0.0)


def _merge_halves(lo, lo_src_half, hi, hi_src_half):
    a = lo if lo_src_half == 0 else pltpu.roll(lo, HALF, axis=1)
    b = hi if hi_src_half == 1 else pltpu.roll(hi, HALF, axis=1)
    return jnp.where(_lane_iota(a.shape) < HALF, a, b)


def _dot_nt(a, b):
    return lax.dot_general(a, b, (((1,), (1,)), ((), ())), preferred_element_type=F32)


def _dot(a, b):
    return jnp.dot(a, b, preferred_element_type=F32)


def _dot_split(a, b):
    ah, bh = a.astype(MXU_DTYPE), b.astype(MXU_DTYPE)
    al, bl = (a - ah.astype(F32)).astype(MXU_DTYPE), (b - bh.astype(F32)).astype(MXU_DTYPE)
    return _dot(ah, bh) + (_dot(al, bh) + _dot(ah, bl))


def _rms(x, g, eps):
    ms = jnp.mean(x * x, axis=-1, keepdims=True)
    return x * lax.rsqrt(ms + eps) * g


def _gelu(x):
    return 0.5 * x * (1.0 + lax.erf(x * math.sqrt(0.5)))


def _diff_lambda(lp, lam_init):
    a = jnp.sum(lp[0:1] * lp[1:2], axis=-1, keepdims=True)
    b = jnp.sum(lp[2:3] * lp[3:4], axis=-1, keepdims=True)
    return jnp.exp(a) - jnp.exp(b) + lam_init


def _rope_tables(pos):
    half = ROT_DIM // 2
    inv = ROPE_THETA ** (-jnp.arange(half, dtype=F32) * 2.0 / ROT_DIM)
    ang = pos.astype(F32)[:, None] * inv[None, :]
    cos, sin = jnp.cos(ang), jnp.sin(ang)
    t = pos.shape[0]
    rest = HEAD_DIM - ROT_DIM
    zh = jnp.zeros((t, half), F32)
    c = jnp.concatenate([cos, cos, jnp.ones((t, rest), F32)], axis=-1)
    sn = jnp.concatenate([-sin, zh, jnp.zeros((t, rest), F32)], axis=-1)
    sp = jnp.concatenate([zh, sin, jnp.zeros((t, rest), F32)], axis=-1)
    rep = LANES // HEAD_DIM
    return tuple(jnp.tile(a, (1, rep)) for a in (c, sn, sp)) + (cos.T, sin.T)


ROWS, COLS, BOTH = "rows", "cols", "both"


def _proj_kernel(x_ref, g_ref, w_ref, cos_ref, sn_ref, sp_ref, cost_ref, sint_ref, *out_refs, splits):
    xn = _rms(x_ref[...], g_ref[...], NORM_EPS).astype(MXU_DTYPE)
    half = ROT_DIM // 2
    outs = iter(out_refs)
    for start, width, rotary, form in splits:
        y = _dot(xn, w_ref[:, start:start + width])
        if form in (ROWS, BOTH):
            o_ref = next(outs)
            cos, sn, sp = cos_ref[...], sn_ref[...], sp_ref[...]
            for c in range(width // LANES):
                yc = y[:, c * LANES:(c + 1) * LANES]
                if rotary:
                    yc = (yc * cos + pltpu.roll(yc, LANES - half, axis=1) * sn
                          + pltpu.roll(yc, half, axis=1) * sp)
                o_ref[:, c * LANES:(c + 1) * LANES] = yc
        if form in (COLS, BOTH):
            o_ref = next(outs)
            yt = jnp.transpose(y)
            if not rotary:
                o_ref[...] = yt
                continue
            cos_t, sin_t = cost_ref[...], sint_ref[...]
            for h in range(width // HEAD_DIM):
                r = h * HEAD_DIM
                x1, x2 = yt[r:r + half], yt[r + half:r + 2 * half]
                o_ref[r:r + half] = x1 * cos_t - x2 * sin_t
                o_ref[r + half:r + 2 * half] = x2 * cos_t + x1 * sin_t
                o_ref[r + 2 * half:r + HEAD_DIM] = yt[r + 2 * half:r + HEAD_DIM]


def _proj_call(x, g, w, tables, splits, tm, seq_rows):
    n, d = x.shape
    nt = tables[0].shape[0]
    assert n % tm == 0 and nt % tm == 0 and seq_rows % tm == 0 and ROT_DIM // 2 == SUBLANES
    ntb = nt // tm
    tps = seq_rows // tm
    row = lambda i: (i, 0)
    fixed = lambda i: (0, 0)
    in_specs = [pl.BlockSpec((tm, d), row), pl.BlockSpec((1, d), fixed), pl.BlockSpec(w.shape, fixed)]
    in_specs += [pl.BlockSpec((tm, LANES), lambda i: (i % ntb, 0))] * 3
    in_specs += [pl.BlockSpec((SUBLANES, tm), lambda i: (0, i % ntb))] * 2
    out_specs, out_shape = [], []
    for _, wd, _, form in splits:
        if form in (ROWS, BOTH):
            out_specs.append(pl.BlockSpec((tm, wd), row))
            out_shape.append(jax.ShapeDtypeStruct((n, wd), F32))
        if form in (COLS, BOTH):
            out_specs.append(pl.BlockSpec((None, wd, tm), lambda i: (i // tps, 0, i % tps)))
            out_shape.append(jax.ShapeDtypeStruct((n // seq_rows, wd, seq_rows), F32))
    return pl.pallas_call(
        functools.partial(_proj_kernel, splits=splits),
        grid=(n // tm,),
        in_specs=in_specs,
        out_specs=out_specs,
        out_shape=out_shape,
        compiler_params=_cparams(("arbitrary",), VMEM_LIMIT),
        name="norm_proj_rope",
    )(x, g.reshape(1, d), w, *tables)


def _ffn_kernel(x_ref, o1_ref, o2_ref, wo1_ref, wo2_ref, g_ref, pa_ref, pb_ref, wa_ref, wb_ref, cwa_ref, cwb_ref,
                cba_ref, cbb_ref, wd_ref, gf_ref, xo_ref, sa_ref, sb_ref,
                x1_s, hn_s, acc_s, ua_s, ub_s, ha_s, hb_s, *, tiles_per_seq, halo, shift, final_norm):
    i, j = pl.program_id(0), pl.program_id(1)
    nj = pl.num_programs(1)
    tm = x_ref.shape[0]

    @pl.when(j == 0)
    def _():
        x1 = (x_ref[...] + _dot(o1_ref[...].astype(MXU_DTYPE), wo1_ref[...])
              + _dot(o2_ref[...].astype(MXU_DTYPE), wo2_ref[...]))
        x1_s[...] = x1
        hn_s[...] = _rms(x1, g_ref[...], NORM_EPS).astype(MXU_DTYPE)
        acc_s[...] = jnp.zeros_like(acc_s)

    first = (i % tiles_per_seq) == 0

    @pl.when(first)
    def _():
        ua_s[0:halo] = pa_ref[...]
        ub_s[0:halo] = pb_ref[...]

    @pl.when(jnp.logical_not(first))
    def _():
        ua_s[0:halo] = ha_s[j]
        ub_s[0:halo] = hb_s[j]

    hn = hn_s[...]
    ua_s[halo:halo + tm] = _dot(hn, wa_ref[...])
    ub_s[halo:halo + tm] = _dot(hn, wb_ref[...])

    def conv(u_s, cw_ref, cb_ref):
        c = cb_ref[...]
        for tap in range(CONV_W):
            off = halo - (CONV_W - 1 - tap) * shift
            c = c + u_s[off:off + tm] * cw_ref[tap:tap + 1]
        return c

    a = conv(ua_s, cwa_ref, cba_ref)
    b = conv(ub_s, cwb_ref, cbb_ref)
    act = (_gelu(a) * b).astype(MXU_DTYPE)
    acc_s[...] += _dot(act, wd_ref[...])

    tail_a = ua_s[tm:tm + halo]
    tail_b = ub_s[tm:tm + halo]
    ha_s[j] = tail_a
    hb_s[j] = tail_b
    sa_ref[...] = tail_a
    sb_ref[...] = tail_b

    @pl.when(j == nj - 1)
    def _():
        xo = x1_s[...] + acc_s[...]
        if final_norm:
            xo = _rms(xo, gf_ref[...], NORM_EPS)
        xo_ref[...] = xo


def _ffn_call(x, o_parts, w_out, g, prev, w_up, conv_w, conv_b, w_down, g_final, *, seq_rows, halo, shift,
              final_norm, tm, tf):
    n, d = x.shape
    dff = w_down.shape[0]
    half = w_out.shape[0] // 2
    assert seq_rows % tm == 0 and dff % tf == 0 and tm >= halo
    tiles_per_seq = seq_rows // tm
    nj = dff // tf
    row = lambda i, j: (i, 0)
    fixed = lambda i, j: (0, 0)
    (o1, c1), (o2, c2) = o_parts
    in_specs = [
        pl.BlockSpec((tm, d), row),
        pl.BlockSpec((tm, half), lambda i, j: (i, c1)), pl.BlockSpec((tm, half), lambda i, j: (i, c2)),
        pl.BlockSpec((half, d), fixed), pl.BlockSpec((half, d), lambda i, j: (1, 0)), pl.BlockSpec((1, d), fixed),
        pl.BlockSpec((None, halo, tf), lambda i, j: (i // tiles_per_seq, 0, j)),
        pl.BlockSpec((None, halo, tf), lambda i, j: (i // tiles_per_seq, 0, nj + j)),
        pl.BlockSpec((d, tf), lambda i, j: (0, j)), pl.BlockSpec((d, tf), lambda i, j: (0, nj + j)),
        pl.BlockSpec((CONV_W, tf), lambda i, j: (0, j)), pl.BlockSpec((CONV_W, tf), lambda i, j: (0, nj + j)),
        pl.BlockSpec((1, tf), lambda i, j: (0, j)), pl.BlockSpec((1, tf), lambda i, j: (0, nj + j)),
        pl.BlockSpec((tf, d), lambda i, j: (j, 0)), pl.BlockSpec((1, d), fixed),
    ]
    tail_spec = pl.BlockSpec((None, halo, tf), lambda i, j: (i, 0, j))
    xo, ta, tb = pl.pallas_call(
        functools.partial(_ffn_kernel, tiles_per_seq=tiles_per_seq, halo=halo, shift=shift,
                          final_norm=final_norm),
        grid=(n // tm, nj),
        in_specs=in_specs,
        out_specs=[pl.BlockSpec((tm, d), row), tail_spec, tail_spec],
        out_shape=[jax.ShapeDtypeStruct((n, d), F32), jax.ShapeDtypeStruct((n // tm, halo, dff), F32),
                   jax.ShapeDtypeStruct((n // tm, halo, dff), F32)],
        scratch_shapes=[
            pltpu.VMEM((tm, d), F32), pltpu.VMEM((tm, d), MXU_DTYPE), pltpu.VMEM((tm, d), F32),
            pltpu.VMEM((halo + tm, tf), F32), pltpu.VMEM((halo + tm, tf), F32),
            pltpu.VMEM((nj, halo, tf), F32), pltpu.VMEM((nj, halo, tf), F32),
        ],
        compiler_params=_cparams(("arbitrary", "arbitrary"), VMEM_LIMIT),
        name="outproj_convffn",
    )(x, o1, o2, w_out, w_out, g.reshape(1, d), prev, prev, w_up, w_up, conv_w, conv_w,
      conv_b.reshape(1, -1), conv_b.reshape(1, -1), w_down, g_final.reshape(1, d))
    last = slice(tiles_per_seq - 1, None, tiles_per_seq)
    return xo, jnp.concatenate([ta[last], tb[last]], axis=-1)


def _flash_update(qs_s, kt, v, m_s, l_s, acc_s, mask_fn):
    tk = kt.shape[1]
    for r0 in range(0, qs_s.shape[0], ROW_CHUNK):
        rs = slice(r0, r0 + ROW_CHUNK)
        s = _dot(qs_s[rs], kt)
        if mask_fn is not None:
            s = mask_fn(r0, s)
        parts = [s[:, i * LANES:(i + 1) * LANES] for i in range(tk // LANES)]
        mx = jnp.max(functools.reduce(jnp.maximum, parts), axis=-1, keepdims=True)
        m_prev = m_s[rs]
        m_new = jnp.maximum(m_prev, mx)
        alpha = jnp.exp2(m_prev - m_new)
        ps = [jnp.exp2(p - m_new) for p in parts]
        l_s[rs] = alpha * l_s[rs] + jnp.sum(functools.reduce(jnp.add, ps), axis=-1, keepdims=True)
        acc_s[rs] = alpha * acc_s[rs] + _dot(jnp.concatenate(ps, axis=1).astype(MXU_DTYPE), v)
        m_s[rs] = m_new


def _causal_mask(tq):
    def mask(r0, s):
        r = (_row_iota(s.shape) + r0) & (tq - 1)
        return jnp.where(_lane_iota(s.shape) <= r, s, NEG)
    return mask


def _diff_finish(acc, l, lam, g, lam_init, rows):
    o = acc[0:rows] / l[0:rows] - lam * (acc[rows:2 * rows] / l[rows:2 * rows])
    return _rms(o, g, SUBLN_EPS) * (1.0 - lam_init)


def _diff_prompt_kernel(q_ref, kt_ref, v_ref, lam_ref, g_ref, o_ref, qs_s, m_s, l_s, acc_s, *, lam_init):
    qi = pl.program_id(2)
    tq = q_ref.shape[0]
    rows = A_GROUP * tq
    q = q_ref[...]
    lane = _lane_iota((tq, LANES))
    for m in range(2):
        for g in range(A_GROUP):
            chunk = q[:, g * LANES:(g + 1) * LANES]
            keep = (lane >= HALF) if m else (lane < HALF)
            r0 = (m * A_GROUP + g) * tq
            qs_s[r0:r0 + tq] = (jnp.where(keep, chunk, 0.0) * QSCALE).astype(MXU_DTYPE)
    m_s[...] = jnp.full_like(m_s, NEG)
    l_s[...] = jnp.zeros_like(l_s)
    acc_s[...] = jnp.zeros_like(acc_s)

    def step(j, mask_fn):
        start = pl.multiple_of(j * tq, tq)
        kt = kt_ref[:, pl.ds(start, tq)].astype(MXU_DTYPE)
        v = v_ref[pl.ds(start, tq), :].astype(MXU_DTYPE)
        _flash_update(qs_s, kt, v, m_s, l_s, acc_s, mask_fn)

    step(qi, _causal_mask(tq))
    lax.fori_loop(0, qi, lambda j, c: (step(j, None), c)[1], 0)

    lam = _diff_lambda(lam_ref[...], lam_init)
    o = _diff_finish(acc_s[...], l_s[...], lam, g_ref[...], lam_init, rows)
    for g in range(A_GROUP):
        o_ref[:, g * LANES:(g + 1) * LANES] = o[g * tq:(g + 1) * tq]


def _diff_prompt_call(qa, kat, va, lam_p, subln_g, lam_init, tq):
    b, s, _ = qa.shape
    assert s % tq == 0 and (tq & (tq - 1)) == 0 and (2 * A_GROUP * tq) % ROW_CHUNK == 0
    kvw = 2 * HEAD_DIM
    qw = A_GROUP * kvw
    rows = 2 * A_GROUP * tq
    return pl.pallas_call(
        functools.partial(_diff_prompt_kernel, lam_init=lam_init),
        grid=(b, A_KV_HEADS, s // tq),
        in_specs=[
            pl.BlockSpec((None, tq, qw), lambda bi, h, i: (bi, i, h)),
            pl.BlockSpec((None, kvw, s), lambda bi, h, i: (bi, h, 0)),
            pl.BlockSpec((None, s, kvw), lambda bi, h, i: (bi, 0, h)),
            pl.BlockSpec(lam_p.shape, lambda bi, h, i: (0, 0)),
            pl.BlockSpec((1, kvw), lambda bi, h, i: (0, 0)),
        ],
        out_specs=pl.BlockSpec((None, tq, qw), lambda bi, h, i: (bi, i, h)),
        out_shape=jax.ShapeDtypeStruct((b, s, A_KV_HEADS * qw), F32),
        scratch_shapes=[pltpu.VMEM((rows, LANES), MXU_DTYPE), pltpu.VMEM((rows, LANES), F32),
                        pltpu.VMEM((rows, LANES), F32), pltpu.VMEM((rows, LANES), F32)],
        compiler_params=_cparams(("arbitrary", "arbitrary", "arbitrary"), VMEM_LIMIT),
        name="diff_attn_prompt",
    )(qa, kat, va, lam_p, subln_g.reshape(1, kvw))


def _topk_select_rows(gate_t, valid, n):
    gm = jnp.where(valid, gate_t, -jnp.inf)
    idx = _row_iota(gm.shape)
    rank = jnp.zeros(gm.shape, F32)
    for i in range(n):
        gi = gm[i:i + 1, :]
        beats = (gi > gm) | ((gi == gm) & (idx > i))
        rank = rank + beats.astype(F32)
    return valid & (rank < MOBA_TOPK)


def _moba_prompt_kernel(q_ref, kt_ref, v_ref, o_ref, qs_s, qf_s, km_s, sel_s, m_s, l_s, acc_s, *, nb):
    qi = pl.program_id(2)
    tq = q_ref.shape[0]
    nbp = -(-nb // SUBLANES) * SUBLANES
    q = q_ref[...]
    for kvp in range(2):
        for g in range(B_GROUP):
            piece = _half_place(q[:, kvp * LANES:(kvp + 1) * LANES], g, kvp)
            r0 = (kvp * B_GROUP + g) * tq
            qf_s[r0:r0 + tq] = piece
            qs_s[r0:r0 + tq] = (piece * QSCALE).astype(MXU_DTYPE)

    @pl.when(qi == 0)
    def _():
        col_block = _lane_iota((LANES, LANES)) & (nb - 1)
        km = jnp.zeros((LANES, LANES), F32)
        for n in range(nb):
            mean_n = jnp.sum(kt_ref[:, n * MOBA_BLOCK:(n + 1) * MOBA_BLOCK], axis=-1, keepdims=True) * (1.0 / MOBA_BLOCK)
            km = jnp.where(col_block == n, mean_n, km)
        km_s[...] = km

    m_s[...] = jnp.full_like(m_s, NEG)
    l_s[...] = jnp.zeros_like(l_s)
    acc_s[...] = jnp.zeros_like(acc_s)

    gate = _dot_split(qf_s[...], km_s[...])
    gate_t = jnp.transpose(gate)[0:nbp]
    sel_t = _topk_select_rows(gate_t, _row_iota(gate_t.shape) < jnp.minimum(qi, nb), nb).astype(F32)
    weights = jnp.left_shift(1, _row_iota(sel_t.shape)).astype(F32)
    bits_row = jnp.sum(sel_t * weights, axis=0, keepdims=True)
    bits = jnp.transpose(jnp.broadcast_to(bits_row, (LANES, gate.shape[0])))
    sel_s[...] = bits.astype(jnp.int32)

    def block_step(back, carry):
        j = qi - back
        start = pl.multiple_of(j * tq, tq)
        kt = kt_ref[:, pl.ds(start, tq)].astype(MXU_DTYPE)
        v = v_ref[pl.ds(start, tq), :].astype(MXU_DTYPE)

        def mask(r0, s):
            picked = jnp.right_shift(sel_s[r0:r0 + ROW_CHUNK], j) & 1
            lane = _lane_iota(picked.shape)
            own_limit = (_row_iota(picked.shape) + r0) & (tq - 1)
            limit = jnp.where(back == 0, own_limit, jnp.where(picked != 0, tq, -1))
            return jnp.concatenate([jnp.where(lane + i * LANES <= limit, s[:, i * LANES:(i + 1) * LANES], NEG)
                                    for i in range(s.shape[1] // LANES)], axis=1)

        _flash_update(qs_s, kt, v, m_s, l_s, acc_s, mask)
        return carry

    lax.fori_loop(0, qi + 1, block_step, 0)

    o = acc_s[...] / l_s[...]
    for kvp in range(2):
        r0 = kvp * B_GROUP * tq
        o_ref[:, kvp * LANES:(kvp + 1) * LANES] = _merge_halves(o[r0:r0 + tq], kvp, o[r0 + tq:r0 + 2 * tq], kvp)


def _moba_prompt_call(qb, kbt, vb):
    b, s, _ = qb.shape
    tq = MOBA_BLOCK
    nb = s // tq
    assert s % tq == 0 and LANES % nb == 0 and nb <= 24 and B_GROUP == 2
    pairs = B_KV_HEADS // 2
    qw = 2 * B_GROUP * HEAD_DIM
    rows = 2 * B_GROUP * tq
    assert rows % ROW_CHUNK == 0
    return pl.pallas_call(
        functools.partial(_moba_prompt_kernel, nb=nb),
        grid=(b, pairs, nb),
        in_specs=[
            pl.BlockSpec((None, tq, qw), lambda bi, h, i: (bi, i, h)),
            pl.BlockSpec((None, LANES, s), lambda bi, h, i: (bi, h, 0)),
            pl.BlockSpec((None, s, LANES), lambda bi, h, i: (bi, 0, h)),
        ],
        out_specs=pl.BlockSpec((None, tq, qw), lambda bi, h, i: (bi, i, h)),
        out_shape=jax.ShapeDtypeStruct((b, s, pairs * qw), F32),
        scratch_shapes=[pltpu.VMEM((rows, LANES), MXU_DTYPE), pltpu.VMEM((rows, LANES), F32),
                        pltpu.VMEM((LANES, LANES), F32), pltpu.VMEM((rows, LANES), jnp.int32),
                        pltpu.VMEM((rows, LANES), F32), pltpu.VMEM((rows, LANES), F32),
                        pltpu.VMEM((rows, LANES), F32)],
        compiler_params=_cparams(("arbitrary", "arbitrary", "arbitrary"), VMEM_LIMIT),
        name="moba_attn_prompt",
    )(qb, kbt, vb)


def _window_q(q, rows):
    pieces = []
    for h in range(C_HEADS):
        chunk = q[:, (h // 2) * LANES:(h // 2 + 1) * LANES]
        pieces.append(_half_place(chunk, h % 2, h // C_GROUP))
    return (jnp.concatenate(pieces, axis=0) * QSCALE).astype(MXU_DTYPE)


def _sink_column(sink_ref, rows):
    return jnp.concatenate([jnp.full((rows, 1), sink_ref[h], F32) for h in range(C_HEADS)], axis=0)


def _window_out(o, rows):
    chunks = []
    for c in range(C_HEADS // 2):
        kv = (2 * c) // C_GROUP
        lo = o[(2 * c) * rows:(2 * c + 1) * rows]
        hi = o[(2 * c + 1) * rows:(2 * c + 2) * rows]
        chunks.append(_merge_halves(lo, kv, hi, kv))
    return chunks


def _sink_softmax_pv(s, sink, v):
    sink = sink * LOG2E
    m = jnp.maximum(jnp.max(s, axis=-1, keepdims=True), sink)
    p = jnp.exp2(s - m)
    den = jnp.sum(p, axis=-1, keepdims=True) + jnp.exp2(sink - m)
    return _dot(p.astype(MXU_DTYPE), v) / den


def _local_prompt_kernel(sink_ref, q_ref, ktp_ref, ktc_ref, vp_ref, vc_ref, o_ref):
    has_prev = pl.program_id(1) > 0
    kt = jnp.concatenate([ktp_ref[...], ktc_ref[...]], axis=1).astype(MXU_DTYPE)
    v = jnp.concatenate([vp_ref[...], vc_ref[...]], axis=0).astype(MXU_DTYPE)
    col = _lane_iota((WINDOW, 2 * WINDOW))
    rel = WINDOW + _row_iota(col.shape) - col
    visible = (rel >= 0) & (rel < WINDOW) & ((col >= WINDOW) | has_prev)
    n_parts = 2 * WINDOW // LANES
    kvs = [h // C_GROUP for h in range(C_HEADS)]
    sinks = [sink_ref[h] * LOG2E for h in range(C_HEADS)]
    qs = [(_half_place(q_ref[:, (h // 2) * LANES:(h // 2 + 1) * LANES], h % 2, kv) * QSCALE).astype(MXU_DTYPE)
          for h, kv in enumerate(kvs)]
    scores = [jnp.where(visible, _dot(qh, kt), NEG) for qh in qs]
    parts = [[s[:, i * LANES:(i + 1) * LANES] for i in range(n_parts)] for s in scores]
    maxes = [jnp.maximum(jnp.max(functools.reduce(jnp.maximum, pt), axis=-1, keepdims=True), sink)
             for pt, sink in zip(parts, sinks)]
    probs = [[jnp.exp2(p - m) for p in pt] for pt, m in zip(parts, maxes)]
    dens = [jnp.sum(functools.reduce(jnp.add, ps), axis=-1, keepdims=True) + jnp.exp2(sink - m)
            for ps, sink, m in zip(probs, sinks, maxes)]
    outs = [_dot(jnp.concatenate(ps, axis=1).astype(MXU_DTYPE), v) / den for ps, den in zip(probs, dens)]
    for c in range(C_HEADS // 2):
        o_ref[:, c * LANES:(c + 1) * LANES] = _merge_halves(outs[2 * c], kvs[2 * c], outs[2 * c + 1], kvs[2 * c])


def _local_prompt_call(q, kt, v, sinks):
    b, s, qw = q.shape
    assert s % WINDOW == 0 and C_KV_HEADS * HEAD_DIM == LANES and WINDOW == LANES
    prev = lambda i: jnp.maximum(i - 1, 0)
    return pl.pallas_call(
        _local_prompt_kernel,
        grid=(b, s // WINDOW),
        in_specs=[
            pl.BlockSpec(memory_space=pltpu.SMEM),
            pl.BlockSpec((None, WINDOW, qw), lambda bi, i: (bi, i, 0)),
            pl.BlockSpec((None, LANES, WINDOW), lambda bi, i: (bi, 0, prev(i))),
            pl.BlockSpec((None, LANES, WINDOW), lambda bi, i: (bi, 0, i)),
            pl.BlockSpec((None, WINDOW, LANES), lambda bi, i: (bi, prev(i), 0)),
            pl.BlockSpec((None, WINDOW, LANES), lambda bi, i: (bi, i, 0)),
        ],
        out_specs=pl.BlockSpec((None, WINDOW, qw), lambda bi, i: (bi, i, 0)),
        out_shape=jax.ShapeDtypeStruct((b, s, qw), F32),
        compiler_params=_cparams(("arbitrary", "arbitrary"), VMEM_LIMIT),
        name="window_attn_prompt",
    )(sinks, q, kt, kt, v, v)


def _local_decode_kernel(sink_ref, q_ref, ck_ref, cv_ref, kn_ref, vn_ref, o_ref, ko_ref, vo_ref):
    nseq, t, _ = q_ref.shape
    buf = ck_ref.shape[1]
    sink = _sink_column(sink_ref, t)
    pad = jnp.zeros((2 * WINDOW - buf - t, LANES), F32)
    for sq in range(nseq):
        kn, vn = kn_ref[sq], vn_ref[sq]
        k = jnp.concatenate([ck_ref[sq], kn, pad], axis=0).astype(MXU_DTYPE)
        v = jnp.concatenate([cv_ref[sq], vn, pad], axis=0).astype(MXU_DTYPE)
        s = _dot_nt(_window_q(q_ref[sq], t), k)
        col = _lane_iota(s.shape)
        rel = buf + (_row_iota(s.shape) & (t - 1)) - col
        s = jnp.where((rel >= 0) & (rel < WINDOW) & (col < buf + t), s, NEG)
        o = _sink_softmax_pv(s, sink, v)
        for c, chunk in enumerate(_window_out(o, t)):
            o_ref[sq, :, c * LANES:(c + 1) * LANES] = chunk
        ko_ref[sq, 0:buf - t] = ck_ref[sq, t:buf]
        ko_ref[sq, buf - t:buf] = kn
        vo_ref[sq, 0:buf - t] = cv_ref[sq, t:buf]
        vo_ref[sq, buf - t:buf] = vn


def _local_decode_call(q, ck, cv, kn, vn, sinks, nseq):
    n, t, qw = q.shape
    buf = ck.shape[1]
    assert n % nseq == 0 and buf + t <= 2 * WINDOW and (t & (t - 1)) == 0 and t % SUBLANES == 0
    blk = lambda shape: pl.BlockSpec((nseq,) + shape, lambda i: (i, 0, 0))
    return pl.pallas_call(
        _local_decode_kernel,
        grid=(n // nseq,),
        in_specs=[pl.BlockSpec(memory_space=pltpu.SMEM), blk((t, qw)), blk((buf, LANES)), blk((buf, LANES)),
                  blk((t, LANES)), blk((t, LANES))],
        out_specs=[blk((t, qw)), blk((buf, LANES)), blk((buf, LANES))],
        out_shape=[jax.ShapeDtypeStruct((n, t, qw), F32), jax.ShapeDtypeStruct((n, buf, LANES), F32),
                   jax.ShapeDtypeStruct((n, buf, LANES), F32)],
        compiler_params=_cparams(("arbitrary",), VMEM_LIMIT),
        name="window_attn_decode",
    )(sinks, q, ck, cv, kn, vn)


def _diff_decode_kernel(pt_ref, q_ref, kn_ref, vn_ref, lam_ref, g_ref, *refs, pp, lam_init):
    kt_refs, v_refs = refs[:pp], refs[pp:2 * pp]
    o_ref = refs[2 * pp]
    qs_s, m_s, l_s, acc_s = refs[2 * pp + 1:]
    gi = pl.program_id(1)
    t = q_ref.shape[0]
    kw, page = kt_refs[0].shape
    gr = 2 * A_GROUP * t

    @pl.when(gi == 0)
    def _():
        q = q_ref[...]
        lane = _lane_iota((t, LANES))
        zero = jnp.zeros((t, LANES), F32)
        for kv in range(A_KV_HEADS):
            for m in range(2):
                for g in range(A_GROUP):
                    c = kv * A_GROUP + g
                    keep = (lane >= HALF) if m else (lane < HALF)
                    piece = jnp.where(keep, q[:, c * LANES:(c + 1) * LANES], 0.0) * QSCALE
                    full = jnp.concatenate([piece if kk == kv else zero for kk in range(A_KV_HEADS)], axis=1)
                    r0 = ((kv * 2 + m) * A_GROUP + g) * t
                    qs_s[r0:r0 + t] = full
        m_s[...] = jnp.full_like(m_s, NEG)
        l_s[...] = jnp.zeros_like(l_s)
        acc_s[...] = jnp.zeros_like(acc_s)

    def update(s, values_of_head):
        parts = [s[:, i * LANES:(i + 1) * LANES] for i in range(s.shape[1] // LANES)]
        mx = jnp.max(functools.reduce(jnp.maximum, parts), axis=-1, keepdims=True)
        m_prev = m_s[...]
        m_new = jnp.maximum(m_prev, mx)
        alpha = jnp.exp2(m_prev - m_new)
        ps = [jnp.exp2(p - m_new) for p in parts]
        l_s[...] = alpha * l_s[...] + jnp.sum(functools.reduce(jnp.add, ps), axis=-1, keepdims=True)
        p = jnp.concatenate(ps, axis=1).astype(MXU_DTYPE)
        pv = jnp.concatenate([_dot(p[kv * gr:(kv + 1) * gr], values_of_head(kv)) for kv in range(A_KV_HEADS)], axis=0)
        acc_s[...] = alpha * acc_s[...] + pv
        m_s[...] = m_new

    def paged_values(kv):
        return jnp.concatenate([vr[pl.ds(kv, page, stride=A_KV_HEADS), :].astype(MXU_DTYPE) for vr in v_refs], axis=0)

    qs = qs_s[...].astype(MXU_DTYPE)
    kt = jnp.concatenate([kr[...].astype(MXU_DTYPE) for kr in kt_refs], axis=1)
    update(_dot(qs, kt), paged_values)

    @pl.when(gi == pl.num_programs(1) - 1)
    def _():
        pad = jnp.zeros((LANES - t, kw), F32)
        kn = jnp.concatenate([kn_ref[...], pad], axis=0).astype(MXU_DTYPE)
        vn = jnp.concatenate([vn_ref[...], pad], axis=0).astype(MXU_DTYPE)
        sn = _dot_nt(qs, kn)
        sn = jnp.where(_lane_iota(sn.shape) <= (_row_iota(sn.shape) & (t - 1)), sn, NEG)
        update(sn, lambda kv: vn[:, kv * LANES:(kv + 1) * LANES])
        lam = _diff_lambda(lam_ref[...], lam_init)
        acc, l = acc_s[...], l_s[...]
        for kv in range(A_KV_HEADS):
            o = _diff_finish(acc[kv * gr:(kv + 1) * gr], l[kv * gr:(kv + 1) * gr], lam, g_ref[...], lam_init,
                             A_GROUP * t)
            for g in range(A_GROUP):
                c = kv * A_GROUP + g
                o_ref[:, c * LANES:(c + 1) * LANES] = o[g * t:(g + 1) * t]


def _paged_specs(layer, pp, n_pages, shape):
    def make(i):
        return pl.BlockSpec((None, None) + shape, lambda b, g, pt: (layer, pt[b * n_pages + g * pp + i], 0, 0))
    return [make(i) for i in range(pp)]


def _diff_decode_call(q, kn, vn, cache_kt, cache_v, layer, pt_flat, n_pages, lam_p, subln_g, lam_init, pp):
    n, t, qw = q.shape
    kw, page = cache_kt.shape[2], cache_kt.shape[3]
    assert n_pages % pp == 0 and (t & (t - 1)) == 0 and t % SUBLANES == 0 and t <= LANES and page == LANES
    assert cache_v.shape[2:] == (page * A_KV_HEADS, 2 * HEAD_DIM)
    rows = A_KV_HEADS * 2 * A_GROUP * t
    per_seq = lambda w: pl.BlockSpec((None, t, w), lambda b, g, pt: (b, 0, 0))
    fixed = lambda shape: pl.BlockSpec(shape, lambda b, g, pt: (0, 0))
    grid_spec = pltpu.PrefetchScalarGridSpec(
        num_scalar_prefetch=1,
        grid=(n, n_pages // pp),
        in_specs=[per_seq(qw), per_seq(kw), per_seq(kw), fixed(lam_p.shape), fixed((1, LANES))]
        + _paged_specs(layer, pp, n_pages, (kw, page))
        + _paged_specs(layer, pp, n_pages, (page * A_KV_HEADS, 2 * HEAD_DIM)),
        out_specs=per_seq(qw),
        scratch_shapes=[pltpu.VMEM((rows, kw), F32), pltpu.VMEM((rows, LANES), F32),
                        pltpu.VMEM((rows, LANES), F32), pltpu.VMEM((rows, LANES), F32)],
    )
    return pl.pallas_call(
        functools.partial(_diff_decode_kernel, pp=pp, lam_init=lam_init),
        grid_spec=grid_spec,
        out_shape=jax.ShapeDtypeStruct((n, t, qw), F32),
        compiler_params=_cparams(("arbitrary", "arbitrary"), VMEM_LIMIT),
        name="diff_attn_decode",
    )(pt_flat, q, kn, vn, lam_p, subln_g.reshape(1, LANES), *([cache_kt] * pp), *([cache_v] * pp))


def _moba_decode_kernel(pt_ref, q_ref, kn_ref, vn_ref, *refs, pp, n_blocks):
    kt_refs, vt_refs = refs[:pp], refs[pp:2 * pp]
    o_ref = refs[2 * pp]
    qf_s, m_s, l_s, g_s, o_s = refs[2 * pp + 1:]
    gi = pl.program_id(1)
    t = q_ref.shape[0]
    kw, page = kt_refs[0].shape
    per_block = MOBA_BLOCK // page
    rows = B_HEADS * t

    @pl.when(gi == 0)
    def _():
        q = q_ref[...]
        zero = jnp.zeros((t, LANES), F32)
        for kv in range(B_KV_HEADS):
            for g in range(B_GROUP):
                piece = _half_place(q[:, kv * LANES:(kv + 1) * LANES], g, kv % 2)
                full = jnp.concatenate([piece if c == kv // 2 else zero for c in range(kw // LANES)], axis=1)
                r0 = (kv * B_GROUP + g) * t
                qf_s[r0:r0 + t] = full
        m_s[...] = jnp.full_like(m_s, NEG)
        l_s[...] = jnp.zeros_like(l_s)
        g_s[...] = jnp.zeros_like(g_s)

    qs = (qf_s[...] * QSCALE).astype(MXU_DTYPE)
    lane = _lane_iota((rows, LANES))
    n_here = pp // per_block
    block_pages = lambda refs, bi: jnp.concatenate(
        [refs[i][...].astype(MXU_DTYPE) for i in range(bi * per_block, (bi + 1) * per_block)], axis=1)

    scores = [_dot(qs, block_pages(kt_refs, bi)) for bi in range(n_here)]
    parts = [[s[:, i * LANES:(i + 1) * LANES] for i in range(MOBA_BLOCK // LANES)] for s in scores]
    maxes = [jnp.max(functools.reduce(jnp.maximum, pt), axis=-1, keepdims=True) for pt in parts]
    probs = [[jnp.exp2(p - mb) for p in pt] for pt, mb in zip(parts, maxes)]
    sums = [jnp.sum(functools.reduce(jnp.add, ps), axis=-1, keepdims=True) for ps in probs]
    gates = [jnp.sum(functools.reduce(jnp.add, pt), axis=-1, keepdims=True) * (1.0 / (MOBA_BLOCK * QSCALE))
             for pt in parts]
    m_all, l_all, g_all = m_s[...], l_s[...], g_s[...]
    for bi in range(n_here):
        jb = gi * n_here + bi
        o_s[jb] = _dot_nt(jnp.concatenate(probs[bi], axis=1).astype(MXU_DTYPE), block_pages(vt_refs, bi))
        here = lane == jb
        m_all = jnp.where(here, maxes[bi], m_all)
        l_all = jnp.where(here, sums[bi], l_all)
        g_all = jnp.where(here, gates[bi], g_all)
    m_s[...] = m_all
    l_s[...] = l_all
    g_s[...] = g_all

    @pl.when(gi == pl.num_programs(1) - 1)
    def _():
        pad = jnp.zeros((LANES - t, kw), F32)
        kn = jnp.concatenate([kn_ref[...], pad], axis=0).astype(MXU_DTYPE)
        vn = jnp.concatenate([vn_ref[...], pad], axis=0).astype(MXU_DTYPE)
        sn = _dot_nt(qs, kn)
        sn = jnp.where(_lane_iota(sn.shape) <= (_row_iota(sn.shape) & (t - 1)), sn, NEG)
        m_own = jnp.max(sn, axis=-1, keepdims=True)
        p_own = jnp.exp2(sn - m_own)
        l_own = jnp.sum(p_own, axis=-1, keepdims=True)
        o_own = _dot(p_own.astype(MXU_DTYPE), vn)

        nbp = -(-n_blocks // SUBLANES) * SUBLANES
        gate_t = jnp.transpose(jnp.concatenate([g_s[...], jnp.zeros((LANES - rows, LANES), F32)], axis=0))[0:nbp]
        sel_t = _topk_select_rows(gate_t, _row_iota(gate_t.shape) < n_blocks, n_blocks).astype(F32)
        sel_t = jnp.concatenate([sel_t, jnp.zeros((LANES - nbp, LANES), F32)], axis=0)
        sel = jnp.transpose(sel_t)[0:rows] > 0.0
        mall = m_s[...]
        mx = jnp.maximum(jnp.max(jnp.where(sel, mall, NEG), axis=-1, keepdims=True), m_own)
        w = jnp.where(sel, jnp.exp2(mall - mx), 0.0)
        w_own = jnp.exp2(m_own - mx)
        den = jnp.sum(w * l_s[...], axis=-1, keepdims=True) + w_own * l_own
        num = w_own * o_own
        for jb in range(n_blocks):
            num = num + w[:, jb:jb + 1] * o_s[jb]
        o = num / den
        for kv in range(B_KV_HEADS):
            r0 = kv * B_GROUP * t
            src = o[r0:r0 + B_GROUP * t, (kv // 2) * LANES:(kv // 2 + 1) * LANES]
            o_ref[:, kv * LANES:(kv + 1) * LANES] = _merge_halves(src[0:t], kv % 2, src[t:2 * t], kv % 2)


def _moba_decode_call(q, kn, vn, cache_kt, cache_vt, layer, pt_flat, n_pages, pp):
    n, t, qw = q.shape
    kw, page = cache_kt.shape[2], cache_kt.shape[3]
    per_block = MOBA_BLOCK // page
    past = n_pages * page
    n_blocks = past // MOBA_BLOCK
    assert MOBA_BLOCK % page == 0 and pp % per_block == 0 and n_pages % pp == 0 and past % MOBA_BLOCK == 0
    assert n_blocks <= LANES and t <= LANES and (t & (t - 1)) == 0 and t % SUBLANES == 0 and B_GROUP == 2
    assert page == LANES and kw == B_KV_HEADS * HEAD_DIM
    rows = B_HEADS * t
    assert rows <= LANES
    per_seq = lambda w: pl.BlockSpec((None, t, w), lambda b, g, pt: (b, 0, 0))
    grid_spec = pltpu.PrefetchScalarGridSpec(
        num_scalar_prefetch=1,
        grid=(n, n_pages // pp),
        in_specs=[per_seq(qw), per_seq(kw), per_seq(kw)]
        + _paged_specs(layer, pp, n_pages, (kw, page)) + _paged_specs(layer, pp, n_pages, (kw, page)),
        out_specs=per_seq(qw),
        scratch_shapes=[pltpu.VMEM((rows, kw), F32),
                        pltpu.VMEM((rows, LANES), F32), pltpu.VMEM((rows, LANES), F32),
                        pltpu.VMEM((rows, LANES), F32), pltpu.VMEM((n_blocks, rows, kw), F32)],
    )
    return pl.pallas_call(
        functools.partial(_moba_decode_kernel, pp=pp, n_blocks=n_blocks),
        grid_spec=grid_spec,
        out_shape=jax.ShapeDtypeStruct((n, t, qw), F32),
        compiler_params=_cparams(("arbitrary", "arbitrary"), VMEM_LIMIT),
        name="moba_attn_decode",
    )(pt_flat, q, kn, vn, *([cache_kt] * pp), *([cache_vt] * pp))


def _global_splits(k_form, vb_form):
    return ((0, A_HEADS * 2 * HEAD_DIM, True, ROWS), (512, A_KV_HEADS * 2 * HEAD_DIM, True, k_form),
            (768, A_KV_HEADS * 2 * HEAD_DIM, False, ROWS), (1024, B_HEADS * HEAD_DIM, True, ROWS),
            (1536, B_KV_HEADS * HEAD_DIM, True, k_form), (1792, B_KV_HEADS * HEAD_DIM, False, vb_form))


def _local_splits(k_form):
    return ((0, C_HEADS * HEAD_DIM, True, ROWS), (1024, C_KV_HEADS * HEAD_DIM, True, k_form),
            (1152, C_KV_HEADS * HEAD_DIM, False, ROWS))


def kernel(x_prompt, x_sample, cache_a_k, cache_a_v, cache_b_k, cache_b_v, state_c_k, state_c_v, state_ffn,
           page_table, norm_mix, w_in_g, w_out_g, diff_lambda, diff_subln, w_in_l, w_out_l, sinks, norm_ffn,
           w_up, conv_w, conv_b, w_down, norm_final):
    bsz, seq, d = x_prompt.shape
    nd, t_dec, _ = x_sample.shape
    depth = norm_mix.shape[0]
    ng, pool, page = cache_a_k.shape[:3]
    n_pages = page_table.shape[1]
    past = n_pages * page
    buf = state_c_k.shape[2]
    dff2 = w_up.shape[2]

    cast = lambda w: w.astype(MXU_DTYPE)
    w_in_g, w_out_g, w_in_l, w_out_l, w_up_c, w_down_c = map(cast, (w_in_g, w_out_g, w_in_l, w_out_l, w_up, w_down))

    kw_a = A_KV_HEADS * 2 * HEAD_DIM
    kw_b = B_KV_HEADS * HEAD_DIM
    cakt = jnp.transpose(cache_a_k, (0, 1, 3, 4, 5, 2)).reshape(ng, pool, kw_a, page)
    cav = cache_a_v.reshape(ng, pool, page * A_KV_HEADS, 2 * HEAD_DIM)
    cbkt = jnp.transpose(cache_b_k, (0, 1, 3, 4, 2)).reshape(ng, pool, kw_b, page)
    cbvt = jnp.transpose(cache_b_v, (0, 1, 3, 4, 2)).reshape(ng, pool, kw_b, page)
    pt_flat = page_table.reshape(-1)

    tab_p = _rope_tables(jnp.arange(seq, dtype=jnp.int32))
    pos_s = past + jnp.repeat(jnp.arange(t_dec, dtype=jnp.int32), nd)
    tab_s = _rope_tables(pos_s)

    n_s = nd * t_dec
    xp = x_prompt.reshape(bsz * seq, d)
    xs = jnp.swapaxes(x_sample, 0, 1).reshape(n_s, d)

    tm_p = _tile(seq, 512)
    tm_s = _tile(n_s, 512)
    tf_p = dff2 // 4 if (dff2 // 4) % LANES == 0 else LANES
    tf_s = 2 * LANES if (dff2 // 2) % (2 * LANES) == 0 else LANES
    halo_p = SUBLANES
    halo_s = (CONV_W - 1) * nd
    pp = 64 if n_pages % 64 == 0 else n_pages
    dec_blk = 8 if nd % 8 == 0 else nd
    r3 = lambda a: a.reshape(bsz, seq, -1)

    def to_seq_major(a):
        return jnp.swapaxes(a.reshape(t_dec, nd, -1), 0, 1)

    def to_pos_major(a):
        return jnp.swapaxes(a, 0, 1).reshape(n_s, -1)

    outs = {k: [] for k in ("akp", "avp", "bkp", "bvp", "ckp", "cvp", "fp", "aks", "avs", "bks", "bvs", "cks", "cvs", "fs")}
    flat = lambda a: a.reshape(bsz * seq, -1)
    for l in range(depth):
        i = l // 2
        if l % 2 == 0:
            lam_init = 0.8 - 0.6 * math.exp(-0.3 * l)
            qa, kat, va, qb, kbt, vb, vbt = _proj_call(xp, norm_mix[l], w_in_g[i], tab_p, _global_splits(COLS, BOTH),
                                                       tm_p, seq)
            oa = _diff_prompt_call(r3(qa), kat, r3(va), diff_lambda[i], diff_subln[i], lam_init, MOBA_BLOCK)
            ob = _moba_prompt_call(r3(qb), kbt, r3(vb))
            op = ((flat(oa), 0), (flat(ob), 0))
            outs["akp"].append(jnp.transpose(kat.reshape(bsz, A_KV_HEADS, 2, HEAD_DIM, seq), (0, 4, 1, 2, 3)))
            outs["avp"].append(va.reshape(bsz, seq, A_KV_HEADS, 2 * HEAD_DIM))
            outs["bkp"].append(jnp.transpose(kbt.reshape(bsz, B_KV_HEADS, HEAD_DIM, seq), (0, 3, 1, 2)))
            outs["bvp"].append(jnp.transpose(vbt.reshape(bsz, B_KV_HEADS, HEAD_DIM, seq), (0, 3, 1, 2)))

            qa, ka, va, qb, kb, vb = map(to_seq_major, _proj_call(xs, norm_mix[l], w_in_g[i], tab_s,
                                                                  _global_splits(ROWS, ROWS), tm_s, n_s))
            oa = _diff_decode_call(qa, ka, va, cakt, cav, i, pt_flat, n_pages, diff_lambda[i], diff_subln[i],
                                   lam_init, pp)
            ob = _moba_decode_call(qb, kb, vb, cbkt, cbvt, i, pt_flat, n_pages, pp)
            os_ = ((to_pos_major(oa), 0), (to_pos_major(ob), 0))
            outs["aks"].append(ka.reshape(nd, t_dec, A_KV_HEADS, 2, HEAD_DIM))
            outs["avs"].append(va.reshape(nd, t_dec, A_KV_HEADS, 2 * HEAD_DIM))
            outs["bks"].append(kb.reshape(nd, t_dec, B_KV_HEADS, HEAD_DIM))
            outs["bvs"].append(vb.reshape(nd, t_dec, B_KV_HEADS, HEAD_DIM))
            w_o = w_out_g[i]
        else:
            q, kt, v = _proj_call(xp, norm_mix[l], w_in_l[i], tab_p, _local_splits(COLS), tm_p, seq)
            op = flat(_local_prompt_call(r3(q), kt, r3(v), sinks[i]))
            op = ((op, 0), (op, 1))
            kt_tail = kt[:, :, seq - buf:].reshape(bsz, C_KV_HEADS, HEAD_DIM, buf)
            outs["ckp"].append(jnp.transpose(kt_tail, (0, 3, 1, 2)))
            outs["cvp"].append(r3(v)[:, seq - buf:].reshape(bsz, buf, C_KV_HEADS, HEAD_DIM))

            q, k, v = map(to_seq_major, _proj_call(xs, norm_mix[l], w_in_l[i], tab_s, _local_splits(ROWS), tm_s, n_s))
            o, ck, cv = _local_decode_call(q, state_c_k[i].reshape(nd, buf, LANES), state_c_v[i].reshape(nd, buf, LANES),
                                           k, v, sinks[i], dec_blk)
            os_ = to_pos_major(o)
            os_ = ((os_, 0), (os_, 1))
            outs["cks"].append(ck.reshape(nd, buf, C_KV_HEADS, HEAD_DIM))
            outs["cvs"].append(cv.reshape(nd, buf, C_KV_HEADS, HEAD_DIM))
            w_o = w_out_l[i]

        final = l == depth - 1
        prev_p = jnp.zeros((bsz, halo_p, dff2), F32)
        xp, tail_p = _ffn_call(xp, op, w_o, norm_ffn[l], prev_p, w_up_c[l], conv_w[l], conv_b[l], w_down_c[l],
                               norm_final, seq_rows=seq, halo=halo_p, shift=1, final_norm=final, tm=tm_p, tf=tf_p)
        outs["fp"].append(tail_p[:, halo_p - (CONV_W - 1):])
        prev_s = jnp.swapaxes(state_ffn[l], 0, 1).reshape(1, halo_s, dff2)
        xs, tail_s = _ffn_call(xs, os_, w_o, norm_ffn[l], prev_s, w_up_c[l], conv_w[l], conv_b[l], w_down_c[l],
                               norm_final, seq_rows=n_s, halo=halo_s, shift=nd, final_norm=final, tm=tm_s, tf=tf_s)
        outs["fs"].append(jnp.swapaxes(tail_s.reshape(CONV_W - 1, nd, dff2), 0, 1))

    st = lambda key: jnp.stack(outs[key])
    y_p = xp.reshape(bsz, seq, d)
    y_s = jnp.swapaxes(xs.reshape(t_dec, nd, d), 0, 1)
    return (y_p, y_s, st("akp"), st("avp"), st("bkp"), st("bvp"), st("ckp"), st("cvp"), st("fp"),
            st("aks"), st("avs"), st("bks"), st("bvs"), st("cks"), st("cvs"), st("fs"))
```
